```python
import jax, jax.numpy as jnp
from jax import lax
import numpy as np

D_MODEL = 1024
BATCH = 2
SEQ = 8192
DEPTH = 4
DEC_BATCH = 128
DEC_SEQ = 1
PAST_LEN = 8192
PAGE_SIZE = 128

HEAD_DIM = 64
SB_HEADS = 4
SB_KV = 2
SB_GROUP = SB_HEADS // SB_KV
MB_HEADS = 4
MB_KV = 2
MB_GROUP = MB_HEADS // MB_KV
MB_BLOCK = 256
MB_TOPK = 3
MLA_HEADS = 4
MLA_Q_RANK = 192
MLA_KV_RANK = 128
MLA_NOPE = 64
MLA_ROPE = 32
MLA_V = 64
ROPE_THETA = 10000.0
LRU_WIDTH = 256
LRU_BLOCKS = 4
LRU_BW = LRU_WIDTH // LRU_BLOCKS
CONV_W = 4
LRU_C = 8.0
N_BRANCH = 4
BRANCH_W = 256
D_FF = 4 * D_MODEL
Q_BLOCK = 128
EPS = 1e-6
IN_SPLITS = (SB_HEADS * HEAD_DIM, SB_KV * HEAD_DIM, SB_KV * HEAD_DIM,
             MB_HEADS * HEAD_DIM, MB_KV * HEAD_DIM, MB_KV * HEAD_DIM,
             MLA_Q_RANK, MLA_KV_RANK, MLA_ROPE,
             LRU_WIDTH, LRU_WIDTH,
             N_BRANCH * D_MODEL)
D_IN = sum(IN_SPLITS)

kernel_name = 'hybrid_sb_moba_mla_rglru_decode_step'

F32 = jnp.float32


def rmsnorm(x, g):
    x32 = x.astype(F32)
    y = x32 * lax.rsqrt(jnp.mean(x32 * x32, axis=-1, keepdims=True) + EPS)
    return (y * g.astype(F32)).astype(x.dtype)


def rope(x, pos):
    half = x.shape[-1] // 2
    inv = jnp.power(ROPE_THETA, -jnp.arange(half, dtype=F32) / half)
    ang = pos.astype(F32)[:, None] * inv
    ang = ang.reshape(ang.shape[:1] + (1,) * (x.ndim - 3) + (half,))
    cos, sin = jnp.cos(ang), jnp.sin(ang)
    x32 = x.astype(F32)
    x1, x2 = x32[..., :half], x32[..., half:]
    return jnp.concatenate([x1 * cos - x2 * sin, x2 * cos + x1 * sin], axis=-1).astype(x.dtype)


def alibi_slopes(n):
    return jnp.exp2(-8.0 * jnp.arange(1, n + 1, dtype=F32) / n)


def sweep_queries(fn, qs, q_pos):
    L = q_pos.shape[0]
    nq = L // Q_BLOCK
    def blk(a):
        return jnp.moveaxis(a.reshape(a.shape[:1] + (nq, Q_BLOCK) + a.shape[2:]), 1, 0)
    xs = (tuple(blk(a) for a in qs), q_pos.reshape(nq, Q_BLOCK))
    out = lax.map(lambda c: fn(*c[0], c[1]), xs)
    out = jnp.moveaxis(out, 0, 1)
    return out.reshape(out.shape[:1] + (L,) + out.shape[3:])


def stickbreak_attend(q, q_pos, k, v, k_pos):
    z = jnp.einsum('bqhgd,bthd->bhgqt', q, k).astype(F32) * (HEAD_DIM ** -0.5)
    causal = k_pos[None, :] < q_pos[:, None]
    log_beta = jax.nn.log_sigmoid(z)
    log_keep = jnp.where(causal, log_beta - z, 0.0)
    later = lax.cumsum(log_keep, axis=z.ndim - 1, reverse=True) - log_keep
    w = jnp.where(causal, jnp.exp(log_beta + later), 0.0)
    return jnp.einsum('bhgqt,bthd->bqhgd', w.astype(v.dtype), v)


def moba_blocks(k, v):
    B, T, H, D = k.shape
    nb = -(-T // MB_BLOCK)
    pad = ((0, 0), (0, nb * MB_BLOCK - T), (0, 0), (0, 0))
    def blk(a):
        return jnp.moveaxis(jnp.pad(a, pad).reshape(B, nb, MB_BLOCK, H, D), 3, 1)
    kbt, vbt = blk(k), blk(v)
    kmean = jnp.mean(kbt.astype(F32), axis=3).astype(k.dtype)
    return kbt, vbt, kmean


def moba_attend(q, q_pos, kbt, vbt, kmean):
    B, Hkv, NB = kbt.shape[:3]
    own = q_pos // MB_BLOCK
    gate = jnp.einsum('bqhgd,bhnd->bqhgn', q, kmean).astype(F32)
    past = jnp.arange(NB)[None, :] < own[:, None]
    gate = jnp.where(past[None, :, None, None, :], gate, -jnp.inf)
    if NB < MB_TOPK:
        gate = jnp.pad(gate, [(0, 0)] * 4 + [(0, MB_TOPK - NB)], constant_values=-jnp.inf)
    top_val, top_idx = lax.top_k(gate, MB_TOPK)
    sel_ok = top_val > -jnp.inf
    own_b = jnp.broadcast_to(own[None, :, None, None, None], top_idx.shape[:-1] + (1,)).astype(top_idx.dtype)
    idx = jnp.concatenate([jnp.minimum(top_idx, NB - 1), own_b], axis=-1)
    blk_ok = jnp.concatenate([sel_ok, jnp.ones_like(sel_ok[..., :1])], axis=-1)
    bi = jnp.arange(B)[:, None, None, None, None]
    hi = jnp.arange(Hkv)[None, None, :, None, None]
    kg = kbt[bi, hi, idx]
    vg = vbt[bi, hi, idx]
    kpos = idx[..., None] * MB_BLOCK + jnp.arange(MB_BLOCK)
    t = q_pos[None, :, None, None, None, None]
    ok = blk_ok[..., None] & (kpos <= t)
    slopes = alibi_slopes(MB_HEADS).reshape(MB_KV, MB_GROUP)[None, None, :, :, None, None]
    s = jnp.einsum('bqhgd,bqhgnkd->bqhgnk', q, kg).astype(F32) * (HEAD_DIM ** -0.5)
    s = jnp.where(ok, s - slopes * (t - kpos).astype(F32), -jnp.inf)
    sh = s.shape
    p = jax.nn.softmax(s.reshape(sh[:4] + (-1,)), axis=-1).reshape(sh).astype(vg.dtype)
    return jnp.einsum('bqhgnk,bqhgnkd->bqhgd', p, vg)


def mla_attend(q_lat, q_pe, q_pos, ckv, kpe, k_pos, w_uv):
    s = (jnp.einsum('bqhr,btr->bhqt', q_lat, ckv)
         + jnp.einsum('bqhe,bte->bhqt', q_pe, kpe)).astype(F32) * ((MLA_NOPE + MLA_ROPE) ** -0.5)
    s = jnp.where(k_pos[None, :] <= q_pos[:, None], s, -jnp.inf)
    p = jax.nn.softmax(s, axis=-1).astype(ckv.dtype)
    o_lat = jnp.einsum('bhqt,btr->bqhr', p, ckv)
    return jnp.einsum('bqhr,rhd->bqhd', o_lat, w_uv)


def causal_conv(x, buf, w, b):
    xp = jnp.concatenate([buf, x], axis=1)
    L = x.shape[1]
    y = sum(xp[:, i:i + L] * w[i] for i in range(CONV_W)) + b
    return y, xp[:, -(CONV_W - 1):]


def rg_lru(x, h0, w_a, b_a, w_x, b_x, lam):
    B, L, W = x.shape
    xb = x.reshape(B, L, LRU_BLOCKS, LRU_BW)
    r = jax.nn.sigmoid((jnp.einsum('blnc,ncd->blnd', xb, w_a).reshape(B, L, W) + b_a).astype(F32))
    i = jax.nn.sigmoid((jnp.einsum('blnc,ncd->blnd', xb, w_x).reshape(B, L, W) + b_x).astype(F32))
    log_a = -LRU_C * r * jax.nn.softplus(-lam.astype(F32))
    a = jnp.exp(log_a)
    u = jnp.sqrt(-jnp.expm1(2.0 * log_a)) * i * x.astype(F32)
    def step(h, au):
        h = au[0] * h + au[1]
        return h, h
    h_last, hs = lax.scan(step, h0.astype(F32), (jnp.moveaxis(a, 1, 0), jnp.moveaxis(u, 1, 0)))
    return jnp.moveaxis(hs, 0, 1).astype(x.dtype), h_last.astype(h0.dtype)


def project(u, pos, lp):
    B, L = u.shape[:2]
    splits = [int(v) for v in np.cumsum(IN_SPLITS)[:-1]]
    qa, ka, va, qb, kb, vb, cq, ckv, kpe, xr, xg, gl = jnp.split(u @ lp['w_in'], splits, axis=-1)
    sb = (qa.reshape(B, L, SB_KV, SB_GROUP, HEAD_DIM), ka.reshape(B, L, SB_KV, HEAD_DIM),
          va.reshape(B, L, SB_KV, HEAD_DIM))
    mb = (qb.reshape(B, L, MB_KV, MB_GROUP, HEAD_DIM), kb.reshape(B, L, MB_KV, HEAD_DIM),
          vb.reshape(B, L, MB_KV, HEAD_DIM))
    q = jnp.einsum('blr,rhe->blhe', rmsnorm(cq, lp['q_norm_g']), lp['w_uq'])
    q_lat = jnp.einsum('blhd,rhd->blhr', q[..., :MLA_NOPE], lp['w_uk'])
    mla = (q_lat, rope(q[..., MLA_NOPE:], pos), rmsnorm(ckv, lp['kv_norm_g']), rope(kpe, pos))
    gates = jax.nn.sigmoid(gl.reshape(B, L, N_BRANCH, D_MODEL) + lp['b_gate'])
    return sb, mb, mla, (xr, xg), gates


def recurrent_branch(xr, xg, buf0, h0, lp):
    xc, buf = causal_conv(xr, buf0, lp['conv_w'], lp['conv_b'])
    hs, h_last = rg_lru(xc, h0, lp['w_rg_a'], lp['b_rg_a'], lp['w_rg_x'], lp['b_rg_x'], lp['lru_lambda'])
    return hs * jax.nn.gelu(xg), h_last, buf


def merge_and_mlp(x, outs, gates, lp):
    o = jnp.stack(outs, axis=2)
    y = jnp.einsum('blkc,kcd->blkd', o, lp['w_branch'])
    x = x + jnp.sum(gates * y, axis=2) @ lp['w_out']
    h = jnp.square(jax.nn.relu(rmsnorm(x, lp['norm2_g']) @ lp['w_mlp1']))
    return x + h @ lp['w_mlp2']


def prompt_layer(x, lp):
    B, L, _ = x.shape
    pos = jnp.arange(L, dtype=jnp.int32)
    u = rmsnorm(x, lp['norm1_g'])
    (qa, ka, va), (qb, kb, vb), (q_lat, q_pe, ckv, kpe), (xr, xg), gates = project(u, pos, lp)
    o_a = sweep_queries(lambda q, qp: stickbreak_attend(q, qp, ka, va, pos), (qa,), pos)
    kbt, vbt, kmean = moba_blocks(kb, vb)
    o_b = sweep_queries(lambda q, qp: moba_attend(q, qp, kbt, vbt, kmean), (qb,), pos)
    o_c = sweep_queries(lambda ql, qe, qp: mla_attend(ql, qe, qp, ckv, kpe, pos, lp['w_uv']), (q_lat, q_pe), pos)
    buf0 = jnp.zeros((B, CONV_W - 1, LRU_WIDTH), x.dtype)
    h0 = jnp.zeros((B, LRU_WIDTH), x.dtype)
    o_d, h_last, buf = recurrent_branch(xr, xg, buf0, h0, lp)
    outs = (o_a.reshape(B, L, -1), o_b.reshape(B, L, -1), o_c.reshape(B, L, -1), o_d)
    return merge_and_mlp(x, outs, gates, lp), (ka, va, kb, vb, ckv, kpe, h_last, buf)


def sample_layer(x, past, h0, buf0, lp):
    B, L, _ = x.shape
    pk_a, pv_a, pk_b, pv_b, p_ckv, p_kpe = past
    P = pk_a.shape[1]
    pos = P + jnp.arange(L, dtype=jnp.int32)
    k_pos = jnp.arange(P + L, dtype=jnp.int32)
    u = rmsnorm(x, lp['norm1_g'])
    (qa, ka, va), (qb, kb, vb), (q_lat, q_pe, ckv, kpe), (xr, xg), gates = project(u, pos, lp)
    cat = lambda p, n: jnp.concatenate([p, n], axis=1)
    o_a = stickbreak_attend(qa, pos, cat(pk_a, ka), cat(pv_a, va), k_pos)
    kbt, vbt, kmean = moba_blocks(cat(pk_b, kb), cat(pv_b, vb))
    o_b = moba_attend(qb, pos, kbt, vbt, kmean)
    o_c = mla_attend(q_lat, q_pe, pos, cat(p_ckv, ckv), cat(p_kpe, kpe), k_pos, lp['w_uv'])
    o_d, h_last, buf = recurrent_branch(xr, xg, buf0, h0, lp)
    outs = (o_a.reshape(B, L, -1), o_b.reshape(B, L, -1), o_c.reshape(B, L, -1), o_d)
    return merge_and_mlp(x, outs, gates, lp), (ka, va, kb, vb, ckv, kpe, h_last, buf)


def paged_rows(cache, l, page_table):
    rows = cache[l, page_table]
    return rows.reshape((page_table.shape[0], -1) + cache.shape[3:])


def setup_inputs(seed: int = 0) -> dict:
    key = jax.random.key(seed)
    ks = list(jax.random.split(key, 40))
    def nrm(shape, scale=1.0):
        return scale * jax.random.normal(ks.pop(), shape, jnp.float32)
    def gain(shape):
        return 1.0 + nrm(shape, 0.02)
    n_pages = PAST_LEN // PAGE_SIZE
    used = DEC_BATCH * n_pages
    n_pool = used + max(1, used // 4)
    x_prompt = nrm((BATCH, SEQ, D_MODEL))
    x_sample = nrm((DEC_BATCH, DEC_SEQ, D_MODEL))
    cache_sb_k = nrm((DEPTH, n_pool, PAGE_SIZE, SB_KV, HEAD_DIM))
    cache_sb_v = nrm((DEPTH, n_pool, PAGE_SIZE, SB_KV, HEAD_DIM))
    cache_moba_k = nrm((DEPTH, n_pool, PAGE_SIZE, MB_KV, HEAD_DIM))
    cache_moba_v = nrm((DEPTH, n_pool, PAGE_SIZE, MB_KV, HEAD_DIM))
    cache_mla_ckv = nrm((DEPTH, n_pool, PAGE_SIZE, MLA_KV_RANK))
    cache_mla_kpe = nrm((DEPTH, n_pool, PAGE_SIZE, MLA_ROPE))
    state_lru_h = nrm((DEPTH, DEC_BATCH, LRU_WIDTH), 0.5)
    state_lru_conv = nrm((DEPTH, DEC_BATCH, CONV_W - 1, LRU_WIDTH))
    page_table = jax.random.permutation(ks.pop(), n_pool)[:used].reshape(DEC_BATCH, n_pages).astype(jnp.int32)
    norm1_g = gain((DEPTH, D_MODEL))
    w_in = nrm((DEPTH, D_MODEL, D_IN), D_MODEL ** -0.5)
    b_gate = nrm((DEPTH, N_BRANCH, D_MODEL), 0.1)
    w_branch = nrm((DEPTH, N_BRANCH, BRANCH_W, D_MODEL), BRANCH_W ** -0.5)
    w_out = nrm((DEPTH, D_MODEL, D_MODEL), D_MODEL ** -0.5)
    q_norm_g = gain((DEPTH, MLA_Q_RANK))
    kv_norm_g = gain((DEPTH, MLA_KV_RANK))
    w_uq = nrm((DEPTH, MLA_Q_RANK, MLA_HEADS, MLA_NOPE + MLA_ROPE), MLA_Q_RANK ** -0.5)
    w_uk = nrm((DEPTH, MLA_KV_RANK, MLA_HEADS, MLA_NOPE), MLA_NOPE ** -0.5)
    w_uv = nrm((DEPTH, MLA_KV_RANK, MLA_HEADS, MLA_V), MLA_KV_RANK ** -0.5)
    conv_w = nrm((DEPTH, CONV_W, LRU_WIDTH), CONV_W ** -0.5)
    conv_b = nrm((DEPTH, LRU_WIDTH), 0.02)
    w_rg_a = nrm((DEPTH, LRU_BLOCKS, LRU_BW, LRU_BW), LRU_BW ** -0.5)
    b_rg_a = nrm((DEPTH, LRU_WIDTH), 0.1)
    w_rg_x = nrm((DEPTH, LRU_BLOCKS, LRU_BW, LRU_BW), LRU_BW ** -0.5)
    b_rg_x = nrm((DEPTH, LRU_WIDTH), 0.1)
    a0 = jax.random.uniform(ks.pop(), (DEPTH, LRU_WIDTH), jnp.float32, 0.9, 0.999)
    s = a0 ** (1.0 / LRU_C)
    lru_lambda = jnp.log(s) - jnp.log1p(-s)
    norm2_g = gain((DEPTH, D_MODEL))
    w_mlp1 = nrm((DEPTH, D_MODEL, D_FF), D_MODEL ** -0.5)
    w_mlp2 = nrm((DEPTH, D_FF, D_MODEL), D_FF ** -0.5)
    final_norm_g = gain((D_MODEL,))
    return {'x_prompt': x_prompt, 'x_sample': x_sample,
            'cache_sb_k': cache_sb_k, 'cache_sb_v': cache_sb_v,
            'cache_moba_k': cache_moba_k, 'cache_moba_v': cache_moba_v,
            'cache_mla_ckv': cache_mla_ckv, 'cache_mla_kpe': cache_mla_kpe,
            'state_lru_h': state_lru_h, 'state_lru_conv': state_lru_conv,
            'page_table': page_table,
            'norm1_g': norm1_g, 'w_in': w_in, 'b_gate': b_gate, 'w_branch': w_branch, 'w_out': w_out,
            'q_norm_g': q_norm_g, 'kv_norm_g': kv_norm_g, 'w_uq': w_uq, 'w_uk': w_uk, 'w_uv': w_uv,
            'conv_w': conv_w, 'conv_b': conv_b, 'w_rg_a': w_rg_a, 'b_rg_a': b_rg_a,
            'w_rg_x': w_rg_x, 'b_rg_x': b_rg_x, 'lru_lambda': lru_lambda,
            'norm2_g': norm2_g, 'w_mlp1': w_mlp1, 'w_mlp2': w_mlp2, 'final_norm_g': final_norm_g}


def reference(x_prompt, x_sample, cache_sb_k, cache_sb_v, cache_moba_k, cache_moba_v,
              cache_mla_ckv, cache_mla_kpe, state_lru_h, state_lru_conv, page_table,
              norm1_g, w_in, b_gate, w_branch, w_out, q_norm_g, kv_norm_g, w_uq, w_uk, w_uv,
              conv_w, conv_b, w_rg_a, b_rg_a, w_rg_x, b_rg_x, lru_lambda,
              norm2_g, w_mlp1, w_mlp2, final_norm_g):
    p_states = [[] for _ in range(8)]
    s_states = [[] for _ in range(8)]
    yp, ys = x_prompt, x_sample
    for l in range(DEPTH):
        lp = {'norm1_g': norm1_g[l], 'w_in': w_in[l], 'b_gate': b_gate[l], 'w_branch': w_branch[l],
              'w_out': w_out[l], 'q_norm_g': q_norm_g[l], 'kv_norm_g': kv_norm_g[l], 'w_uq': w_uq[l],
              'w_uk': w_uk[l], 'w_uv': w_uv[l], 'conv_w': conv_w[l], 'conv_b': conv_b[l],
              'w_rg_a': w_rg_a[l], 'b_rg_a': b_rg_a[l], 'w_rg_x': w_rg_x[l], 'b_rg_x': b_rg_x[l],
              'lru_lambda': lru_lambda[l], 'norm2_g': norm2_g[l], 'w_mlp1': w_mlp1[l], 'w_mlp2': w_mlp2[l]}
        yp, new_p = prompt_layer(yp, lp)
        past = (paged_rows(cache_sb_k, l, page_table), paged_rows(cache_sb_v, l, page_table),
                paged_rows(cache_moba_k, l, page_table), paged_rows(cache_moba_v, l, page_table),
                paged_rows(cache_mla_ckv, l, page_table), paged_rows(cache_mla_kpe, l, page_table))
        ys, new_s = sample_layer(ys, past, state_lru_h[l], state_lru_conv[l], lp)
        for lst, a in zip(p_states, new_p):
            lst.append(a)
        for lst, a in zip(s_states, new_s):
            lst.append(a)
    y_prompt = rmsnorm(yp, final_norm_g)
    y_sample = rmsnorm(ys, final_norm_g)
    sbk_p, sbv_p, mbk_p, mbv_p, ckv_p, kpe_p, h_p, cv_p = [jnp.stack(s) for s in p_states]
    sbk_s, sbv_s, mbk_s, mbv_s, ckv_s, kpe_s, h_s, cv_s = [jnp.stack(s) for s in s_states]
    return (y_prompt, y_sample, sbk_p, sbv_p, mbk_p, mbv_p, ckv_p, kpe_p, h_p, cv_p,
            sbk_s, sbv_s, mbk_s, mbv_s, ckv_s, kpe_s, h_s, cv_s)
```

```python
import functools

import jax
import jax.numpy as jnp
from jax import lax
from jax.experimental import pallas as pl
from jax.experimental.pallas import tpu as pltpu

F32 = jnp.float32
BF16 = jnp.bfloat16

D_MODEL = 1024
HEAD_DIM = 64
N_HEADS = 4
N_KV = 2
MB_BLOCK = 256
MB_TOPK = 3
MLA_Q_RANK = 192
MLA_KV_RANK = 128
MLA_NOPE = 64
MLA_ROPE = 32
MLA_V = 64
ROPE_THETA = 10000.0
LRU_WIDTH = 256
LRU_BLOCKS = 4
CONV_W = 4
LRU_C = 8.0
N_BRANCH = 4
BRANCH_W = 256
D_FF = 4 * D_MODEL
EPS = 1e-6
PAGE = 128

LANES = 128
ROWS16 = 16
NEG_BIG = -1e30
SB_DEAD = -104.0
VMEM_LIMIT = 56 * 1024 * 1024

NT = (((1,), (1,)), ((), ()))


def _dot(a, b):
    return jnp.dot(a, b, preferred_element_type=F32)


def _dot_nt(a, b):
    return lax.dot_general(a, b, NT, preferred_element_type=F32)


def _split2(x):
    hi = x.astype(BF16)
    lo = (x - hi.astype(F32)).astype(BF16)
    return hi, lo


def _split3(x):
    hi = x.astype(BF16)
    r = x - hi.astype(F32)
    mid = r.astype(BF16)
    lo = (r - mid.astype(F32)).astype(BF16)
    return hi, mid, lo


def _rms(x, g, n):
    ms = jnp.sum(x * x, axis=-1, keepdims=True) * (1.0 / n)
    return x * lax.rsqrt(ms + EPS) * g


def _softplus_neg_abs(z):
    return jnp.log(1.0 + jnp.exp(-jnp.abs(z)))


def _gelu(x):
    return 0.5 * x * (1.0 + jnp.tanh(0.7978845608028654 * (x + 0.044715 * x * x * x)))


def _sigmoid(x):
    return 1.0 / (1.0 + jnp.exp(-x))


def _cparams(sem):
    return pltpu.CompilerParams(dimension_semantics=sem, vmem_limit_bytes=VMEM_LIMIT)


def _const_spec(shape):
    nd = len(shape)
    return pl.BlockSpec(shape, lambda *_: (0,) * nd)


def _proj_body(x_ref, g1_ref, w_ref, wlo_ref, qg_ref, kvg_ref, wuqn_ref, wuqp_ref, wuk_ref, cos_ref, sin_ref,
               qa_ref, qb_ref, qbl_ref, ka_ref, va_ref, kb_ref, vb_ref, kab_ref, vab_ref, kbb_ref, vbb_ref,
               qc_ref, qpe_ref, kc_ref, ckv_ref, kpe_ref, xrg_ref, *rest, tm, with_kmean):
    x = x_ref[...]
    u32 = _rms(x, g1_ref[...], D_MODEL)
    u, u_lo = _split2(u32)
    acc = _dot(u, w_ref[...])
    fine = _dot(u_lo, w_ref[:, 512:896]) + _dot(u, wlo_ref[...])
    lane = lax.broadcasted_iota(jnp.int32, (tm, LANES), 1)
    lo = lane < HEAD_DIM

    def pad_heads(q):
        v0, v1 = q[:, :LANES], q[:, LANES:]
        r0, r1 = pltpu.roll(v0, HEAD_DIM, 1), pltpu.roll(v1, HEAD_DIM, 1)
        z = jnp.zeros_like(v0)
        heads = [jnp.where(lo, v0, z), jnp.where(lo, r0, z), jnp.where(lo, z, r1), jnp.where(lo, z, v1)]
        return jnp.concatenate(heads, axis=1) * (HEAD_DIM ** -0.5)

    qa_ref[...] = pad_heads(acc[:, 0:256]).astype(BF16)
    ka, va = acc[:, 256:384], acc[:, 384:512]
    qb_hi, qb_lo = _split2(pad_heads(acc[:, 512:768] + fine[:, :256]))
    qb_ref[...] = qb_hi
    qbl_ref[...] = qb_lo
    kb, vb = acc[:, 768:896] + fine[:, 256:], acc[:, 896:1024]
    ka_ref[...] = ka
    va_ref[...] = va
    kb_ref[...] = kb
    vb_ref[...] = vb
    kab_ref[...] = ka.astype(BF16)
    vab_ref[...] = va.astype(BF16)
    kbb_ref[...] = kb.astype(BF16)
    vbb_ref[...] = vb.astype(BF16)
    if with_kmean:
        kmean_ref = rest[0]
        kmean_ref[...] = jnp.sum(kb.reshape(tm // MB_BLOCK, MB_BLOCK, LANES), axis=1) * (1.0 / MB_BLOCK)

    cqn = _rms(acc[:, 1024:1280], qg_ref[...], MLA_Q_RANK).astype(BF16)
    q_nope = _dot(cqn, wuqn_ref[...]).astype(BF16)
    q_pe = _dot(cqn, wuqp_ref[...])
    q_lat = _dot(q_nope, wuk_ref[...])
    cos, sin = cos_ref[...], sin_ref[...]
    first_half = (lane % MLA_ROPE) < (MLA_ROPE // 2)

    def rope(v):
        swapped = jnp.where(first_half, pltpu.roll(v, LANES - MLA_ROPE // 2, 1), pltpu.roll(v, MLA_ROPE // 2, 1))
        return v * cos + swapped * sin

    scale = (MLA_NOPE + MLA_ROPE) ** -0.5
    q_pe = rope(q_pe) * scale
    qpe_ref[...] = q_pe.astype(BF16)
    pieces = []
    for h in range(N_HEADS):
        pieces.append(q_lat[:, h * LANES:(h + 1) * LANES] * scale)
        pieces.append(jnp.where(lane // MLA_ROPE == h, q_pe, 0.0))
    qc_ref[...] = jnp.concatenate(pieces, axis=1).astype(BF16)

    ckvn = _rms(acc[:, 1280:1408], kvg_ref[...], MLA_KV_RANK)
    kpe4 = rope(acc[:, 1408:1536])
    ckv_ref[...] = ckvn
    kpe_ref[...] = kpe4[:, :MLA_ROPE]
    kc_ref[...] = jnp.concatenate([ckvn, kpe4], axis=1).astype(BF16)
    xrg_ref[...] = acc[:, 1536:2048]


def _proj(x, tables, lw, *, tm, with_kmean):
    n = x.shape[0]
    nt = n // tm
    cos_t, sin_t = tables
    npos = cos_t.shape[0] // tm
    row = lambda w: pl.BlockSpec((tm, w), lambda i: (i, 0))
    tab = pl.BlockSpec((tm, LANES), lambda i: (i % npos, 0))
    out_shape = [
        jax.ShapeDtypeStruct((n, 512), BF16), jax.ShapeDtypeStruct((n, 512), BF16),
        jax.ShapeDtypeStruct((n, 512), BF16),
        jax.ShapeDtypeStruct((n, LANES), F32), jax.ShapeDtypeStruct((n, LANES), F32),
        jax.ShapeDtypeStruct((n, LANES), F32), jax.ShapeDtypeStruct((n, LANES), F32),
        jax.ShapeDtypeStruct((n, LANES), BF16), jax.ShapeDtypeStruct((n, LANES), BF16),
        jax.ShapeDtypeStruct((n, LANES), BF16), jax.ShapeDtypeStruct((n, LANES), BF16),
        jax.ShapeDtypeStruct((n, 1024), BF16), jax.ShapeDtypeStruct((n, LANES), BF16),
        jax.ShapeDtypeStruct((n, 256), BF16), jax.ShapeDtypeStruct((n, LANES), F32),
        jax.ShapeDtypeStruct((n, MLA_ROPE), F32), jax.ShapeDtypeStruct((n, 512), F32),
    ]
    out_specs = [row(512), row(512), row(512), row(LANES), row(LANES), row(LANES), row(LANES),
                 row(LANES), row(LANES), row(LANES), row(LANES),
                 row(1024), row(LANES), row(256), row(LANES), row(MLA_ROPE), row(512)]
    if with_kmean:
        nb = tm // MB_BLOCK
        out_shape.append(jax.ShapeDtypeStruct((nt, nb, LANES), F32))
        out_specs.append(pl.BlockSpec((None, nb, LANES), lambda i: (i, 0, 0)))
    return pl.pallas_call(
        functools.partial(_proj_body, tm=tm, with_kmean=with_kmean),
        grid=(nt,),
        in_specs=[row(D_MODEL), _const_spec((1, D_MODEL)), _const_spec((D_MODEL, 2048)),
                  _const_spec((D_MODEL, 384)),
                  _const_spec((1, 256)), _const_spec((1, LANES)), _const_spec((256, 256)),
                  _const_spec((256, LANES)), _const_spec((256, 512)), tab, tab],
        out_specs=out_specs, out_shape=out_shape,
        compiler_params=_cparams(("parallel",)), name="proj",
    )(x, lw["g1"], lw["w_proj"], lw["w_proj_lo"], lw["qg"], lw["kvg"], lw["wuq_nope"], lw["wuq_pe"], lw["wuk_bd"],
      cos_t, sin_t)


def _head_lanes_out(acc, h, tq):
    a0, a1 = acc[:tq], acc[tq:]
    r0, r1 = pltpu.roll(a0, HEAD_DIM, 1), pltpu.roll(a1, HEAD_DIM, 1)
    lo = lax.broadcasted_iota(jnp.int32, (tq, LANES), 1) < HEAD_DIM
    first = jnp.where(h == 0, a0, r0)
    second = jnp.where(h == 0, r1, a1)
    return jnp.where(lo, first, second)


def _sb_prompt_body(q_ref, k_ref, v_ref, o_ref, *, tq, tk):
    h = pl.program_id(1)
    qi = pl.program_id(2)
    qblk = q_ref[...]
    q = jnp.concatenate([qblk[:, :LANES], qblk[:, LANES:]], axis=0)
    rows = 2 * tq
    t_pos = qi * tq + lax.broadcasted_iota(jnp.int32, (rows, tk), 0) % tq
    s_loc = lax.broadcasted_iota(jnp.int32, (rows, tk), 1)
    suffix = (lax.broadcasted_iota(jnp.int32, (tk, tk), 0) > lax.broadcasted_iota(jnp.int32, (tk, tk), 1)
              ).astype(BF16)

    def block(kb, carry, acc, masked):
        start = pl.multiple_of(kb * tk, tk)
        z = _dot_nt(q, k_ref[pl.ds(start, tk), :])
        sp = _softplus_neg_abs(z)
        log_keep = jnp.minimum(-z, 0.0) - sp
        log_beta = jnp.minimum(z, 0.0) - sp
        if masked:
            causal = (start + s_loc) < t_pos
            log_keep = jnp.where(causal, log_keep, 0.0)
        hi, lo = _split2(log_keep)
        later = _dot(hi, suffix) + _dot(lo, suffix)
        w = jnp.exp(log_beta + later + carry)
        if masked:
            w = jnp.where(causal, w, 0.0)
        acc = acc + _dot(w.astype(BF16), v_ref[pl.ds(start, tk), :])
        carry = carry + later[:, :1] + log_keep[:, :1]
        return carry, acc

    carry = jnp.zeros((rows, 1), F32)
    acc = jnp.zeros((rows, LANES), F32)
    n_diag = tq // tk
    for d in range(n_diag):
        carry, acc = block(qi * n_diag + (n_diag - 1 - d), carry, acc, True)

    def cond(c):
        kb, carry, _ = c
        return jnp.logical_and(kb >= 0, jnp.max(carry) > SB_DEAD)

    def step(c):
        kb, carry, acc = c
        carry, acc = block(kb, carry, acc, False)
        return kb - 1, carry, acc

    _, _, acc = lax.while_loop(cond, step, (qi * n_diag - 1, carry, acc))
    o_ref[...] = _head_lanes_out(acc, h, tq).astype(o_ref.dtype)


def _sb_prompt(qa_pad, kab, vab, *, batch, seq, tq, tk):
    nq = seq // tq
    return pl.pallas_call(
        functools.partial(_sb_prompt_body, tq=tq, tk=tk),
        grid=(batch, N_KV, nq),
        in_specs=[pl.BlockSpec((tq, 256), lambda b, h, i: (b * nq + i, h)),
                  pl.BlockSpec((seq, LANES), lambda b, h, i: (b, 0)),
                  pl.BlockSpec((seq, LANES), lambda b, h, i: (b, 0))],
        out_specs=pl.BlockSpec((tq, LANES), lambda b, h, i: (b * nq + i, h)),
        out_shape=jax.ShapeDtypeStruct((batch * seq, 256), BF16),
        compiler_params=_cparams(("parallel", "parallel", "arbitrary")), name="sb_prompt",
    )(qa_pad, kab, vab)


def _top3_mask(gate, valid):
    nb = gate.shape[1]
    idx = lax.broadcasted_iota(jnp.int32, gate.shape, 1).astype(F32)
    g = jnp.where(valid, gate, -jnp.inf)
    sel = jnp.zeros(gate.shape, jnp.bool_)
    for _ in range(MB_TOPK):
        m = jnp.max(g, axis=1, keepdims=True)
        first = jnp.min(jnp.where(g == m, idx, float(nb)), axis=1, keepdims=True)
        pick = idx == first
        sel = jnp.logical_or(sel, pick)
        g = jnp.where(pick, -jnp.inf, g)
    return jnp.logical_and(sel, valid)


def _alibi_slope(head):
    return jnp.exp2(-2.0 * (head + 1).astype(F32))


def _moba_prompt_body(q_ref, ql_ref, k_ref, v_ref, km_ref, o_ref, *, tq, nb_pad):
    h = pl.program_id(1)
    qi = pl.program_id(2)
    qblk = q_ref[...]
    q = jnp.concatenate([qblk[:, :LANES], qblk[:, LANES:]], axis=0)
    q_lo = jnp.concatenate([ql_ref[:, :LANES], ql_ref[:, LANES:]], axis=0)
    rows = 2 * tq
    tk = MB_BLOCK
    row = lax.broadcasted_iota(jnp.int32, (rows, tk), 0)
    t_loc = row % tq
    s_loc = lax.broadcasted_iota(jnp.int32, (rows, tk), 1)
    slope = _alibi_slope(2 * h + lax.broadcasted_iota(jnp.int32, (rows, 1), 0) // tq)

    km_hi, km_lo = _split2(km_ref[...])
    gate = (_dot_nt(q, km_hi) + _dot_nt(q_lo, km_hi) + _dot_nt(q, km_lo)) * (HEAD_DIM ** 0.5)
    blk = lax.broadcasted_iota(jnp.int32, (rows, nb_pad), 1)
    sel = _top3_mask(gate, blk < qi).astype(F32)

    own = pl.multiple_of(qi * tk, tk)
    s = _dot_nt(q, k_ref[pl.ds(own, tk), :]) - slope * (t_loc - s_loc).astype(F32)
    s = jnp.where(s_loc <= t_loc, s, NEG_BIG)
    m = jnp.max(s, axis=1, keepdims=True)
    p = jnp.exp(s - m)
    l = jnp.sum(p, axis=1, keepdims=True)
    acc = _dot(p.astype(BF16), v_ref[pl.ds(own, tk), :])

    def step(n, c):
        m, l, acc = c
        start = pl.multiple_of(n * tk, tk)
        chosen = jnp.sum(jnp.where(blk == n, sel, 0.0), axis=1, keepdims=True) > 0.5
        dist = (qi - n) * tk + t_loc - s_loc
        s = _dot_nt(q, k_ref[pl.ds(start, tk), :]) - slope * dist.astype(F32)
        s = jnp.where(chosen, s, NEG_BIG)
        m_new = jnp.maximum(m, jnp.max(s, axis=1, keepdims=True))
        alpha = jnp.exp(m - m_new)
        p = jnp.exp(s - m_new)
        l = alpha * l + jnp.sum(p, axis=1, keepdims=True)
        acc = alpha * acc + _dot(p.astype(BF16), v_ref[pl.ds(start, tk), :])
        return m_new, l, acc

    m, l, acc = lax.fori_loop(0, qi, step, (m, l, acc))
    o_ref[...] = _head_lanes_out(acc / l, h, tq).astype(o_ref.dtype)


def _moba_prompt(qb_pad, qb_lo, kbb, vbb, kmean, *, batch, seq):
    tq = MB_BLOCK
    nq = seq // tq
    nb_pad = kmean.shape[1]
    return pl.pallas_call(
        functools.partial(_moba_prompt_body, tq=tq, nb_pad=nb_pad),
        grid=(batch, N_KV, nq),
        in_specs=[pl.BlockSpec((tq, 256), lambda b, h, i: (b * nq + i, h)),
                  pl.BlockSpec((tq, 256), lambda b, h, i: (b * nq + i, h)),
                  pl.BlockSpec((seq, LANES), lambda b, h, i: (b, 0)),
                  pl.BlockSpec((seq, LANES), lambda b, h, i: (b, 0)),
                  pl.BlockSpec((None, nb_pad, LANES), lambda b, h, i: (b, 0, 0))],
        out_specs=pl.BlockSpec((tq, LANES), lambda b, h, i: (b * nq + i, h)),
        out_shape=jax.ShapeDtypeStruct((batch * seq, 256), BF16),
        compiler_params=_cparams(("parallel", "parallel", "arbitrary")), name="moba_prompt",
    )(qb_pad, qb_lo, kbb, vbb, kmean)


def _mla_prompt_body(q_ref, kc_ref, wuv_ref, o_ref, *, tq, tk):
    qi = pl.program_id(1)
    qblk = q_ref[...]
    q = jnp.concatenate([qblk[:, h * 256:(h + 1) * 256] for h in range(N_HEADS)], axis=0)
    rows = N_HEADS * tq
    t_loc = lax.broadcasted_iota(jnp.int32, (rows, tk), 0) % tq
    s_loc = lax.broadcasted_iota(jnp.int32, (rows, tk), 1)

    def block(kb, c, masked):
        m, l, acc = c
        start = pl.multiple_of(kb * tk, tk)
        kblk = kc_ref[pl.ds(start, tk), :]
        s = _dot_nt(q, kblk)
        if masked:
            s = jnp.where(start + s_loc <= qi * tq + t_loc, s, NEG_BIG)
        m_new = jnp.maximum(m, jnp.max(s, axis=1, keepdims=True))
        alpha = jnp.exp(m - m_new)
        p = jnp.exp(s - m_new)
        l = alpha * l + jnp.sum(p, axis=1, keepdims=True)
        acc = alpha * acc + _dot(p.astype(BF16), kblk[:, :MLA_KV_RANK])
        return m_new, l, acc

    diag = (qi * tq) // tk
    init = (jnp.full((rows, 1), NEG_BIG, F32), jnp.zeros((rows, 1), F32), jnp.zeros((rows, MLA_KV_RANK), F32))
    c = block(diag, init, True)
    m, l, acc = lax.fori_loop(0, diag, lambda kb, c: block(kb, c, False), c)
    o_lat = (acc / l).astype(BF16)
    outs = [_dot(o_lat[h * tq:(h + 1) * tq], wuv_ref[h]) for h in range(N_HEADS)]
    o_ref[...] = jnp.concatenate(outs, axis=1).astype(o_ref.dtype)


def _mla_prompt(qc, kc, wuv, *, batch, seq, tq, tk):
    nq = seq // tq
    return pl.pallas_call(
        functools.partial(_mla_prompt_body, tq=tq, tk=tk),
        grid=(batch, nq),
        in_specs=[pl.BlockSpec((tq, 1024), lambda b, i: (b * nq + i, 0)),
                  pl.BlockSpec((seq, 256), lambda b, i: (b, 0)),
                  _const_spec((N_HEADS, MLA_KV_RANK, MLA_V))],
        out_specs=pl.BlockSpec((tq, 256), lambda b, i: (b * nq + i, 0)),
        out_shape=jax.ShapeDtypeStruct((batch * seq, 256), BF16),
        compiler_params=_cparams(("parallel", "arbitrary")), name="mla_prompt",
    )(qc, kc, wuv)


def _lru_gates(xc, wa_ref, ba_ref, wx_ref, bx_ref, lam_ref):
    xb = xc.astype(BF16)
    r = _sigmoid(_dot(xb, wa_ref[...]) + ba_ref[...])
    i = _sigmoid(_dot(xb, wx_ref[...]) + bx_ref[...])
    lam = lam_ref[...]
    softplus_neg_lam = jnp.maximum(-lam, 0.0) + _softplus_neg_abs(lam)
    log_a = -LRU_C * r * softplus_neg_lam
    a = jnp.exp(log_a)
    th = jnp.tanh(log_a)
    u = jnp.sqrt(-2.0 * th / (1.0 - th)) * i * xc
    return a, u


def _lru_prompt_body(xrg_ref, cw_ref, cb_ref, wa_ref, ba_ref, wx_ref, bx_ref, lam_ref,
                     od_ref, hlast_ref, xpad, hprev, *, tt):
    t = pl.program_id(1)

    @pl.when(t == 0)
    def _():
        xpad[0:8, :] = jnp.zeros((8, LRU_WIDTH), F32)
        hprev[...] = jnp.zeros((1, LRU_WIDTH), F32)

    x = xrg_ref[:, :LRU_WIDTH]
    xg = xrg_ref[:, LRU_WIDTH:]
    xpad[8:8 + tt, :] = x
    xc = cb_ref[...] + cw_ref[CONV_W - 1:CONV_W, :] * x
    for i in range(CONV_W - 1):
        back = CONV_W - 1 - i
        xc = xc + cw_ref[i:i + 1, :] * xpad[8 - back:8 - back + tt, :]
    xpad[0:8, :] = x[tt - 8:, :]

    a, u = _lru_gates(xc, wa_ref, ba_ref, wx_ref, bx_ref, lam_ref)
    row = lax.broadcasted_iota(jnp.int32, (tt, LRU_WIDTH), 0)
    d = 1
    while d < tt:
        keep = row >= d
        a_sh = jnp.where(keep, pltpu.roll(a, d, 0), 1.0)
        u_sh = jnp.where(keep, pltpu.roll(u, d, 0), 0.0)
        u = u + a * u_sh
        a = a * a_sh
        d *= 2
    hs = a * hprev[...] + u
    hprev[...] = hs[tt - 1:, :]
    hlast_ref[...] = hs[tt - 1:, :]
    od_ref[...] = (hs * _gelu(xg)).astype(od_ref.dtype)


def _lru_prompt(xrg, lw, *, batch, seq, tt):
    nt = seq // tt
    vec = _const_spec((1, LRU_WIDTH))
    mat = _const_spec((LRU_WIDTH, LRU_WIDTH))
    return pl.pallas_call(
        functools.partial(_lru_prompt_body, tt=tt),
        grid=(batch, nt),
        in_specs=[pl.BlockSpec((tt, 512), lambda b, t: (b * nt + t, 0)),
                  _const_spec((CONV_W, LRU_WIDTH)), vec, mat, vec, mat, vec, vec],
        out_specs=[pl.BlockSpec((tt, LRU_WIDTH), lambda b, t: (b * nt + t, 0)),
                   pl.BlockSpec((None, 1, LRU_WIDTH), lambda b, t: (b, 0, 0))],
        out_shape=[jax.ShapeDtypeStruct((batch * seq, LRU_WIDTH), BF16),
                   jax.ShapeDtypeStruct((batch, 1, LRU_WIDTH), F32)],
        scratch_shapes=[pltpu.VMEM((tt + 8, LRU_WIDTH), F32), pltpu.VMEM((1, LRU_WIDTH), F32)],
        compiler_params=_cparams(("parallel", "arbitrary")), name="lru_prompt",
    )(xrg, lw["conv_w"], lw["conv_b"], lw["w_rg_a"], lw["b_rg_a"], lw["w_rg_x"], lw["b_rg_x"], lw["lam"])


def _lru_sample_body(xrg_ref, buf_ref, h0_ref, cw_ref, cb_ref, wa_ref, ba_ref, wx_ref, bx_ref, lam_ref,
                     od_ref, h_ref):
    x = xrg_ref[:, :LRU_WIDTH]
    xg = xrg_ref[:, LRU_WIDTH:]
    xc = cb_ref[...] + cw_ref[CONV_W - 1:CONV_W, :] * x
    for i in range(CONV_W - 1):
        xc = xc + cw_ref[i:i + 1, :] * buf_ref[i]
    a, u = _lru_gates(xc, wa_ref, ba_ref, wx_ref, bx_ref, lam_ref)
    h = a * h0_ref[...] + u
    h_ref[...] = h
    od_ref[...] = (h * _gelu(xg)).astype(od_ref.dtype)


def _lru_sample(xrg, buf_t, h0, lw):
    s = xrg.shape[0]
    vec = _const_spec((1, LRU_WIDTH))
    mat = _const_spec((LRU_WIDTH, LRU_WIDTH))
    return pl.pallas_call(
        _lru_sample_body,
        grid=(1,),
        in_specs=[_const_spec((s, 512)), _const_spec((CONV_W - 1, s, LRU_WIDTH)), _const_spec((s, LRU_WIDTH)),
                  _const_spec((CONV_W, LRU_WIDTH)), vec, mat, vec, mat, vec, vec],
        out_specs=[_const_spec((s, LRU_WIDTH)), _const_spec((s, LRU_WIDTH))],
        out_shape=[jax.ShapeDtypeStruct((s, LRU_WIDTH), BF16), jax.ShapeDtypeStruct((s, LRU_WIDTH), F32)],
        compiler_params=_cparams(("arbitrary",)), name="lru_sample",
    )(xrg, buf_t, h0, lw["conv_w"], lw["conv_b"], lw["w_rg_a"], lw["b_rg_a"], lw["w_rg_x"], lw["b_rg_x"],
      lw["lam"])


def _merge_body(x_ref, oa_ref, ob_ref, oc_ref, od_ref, g1_ref, wgl_ref, bg_ref, wbr_ref, wout_ref, y_ref):
    x = x_ref[...]
    u = _rms(x, g1_ref[...], D_MODEL).astype(BF16)
    s = jnp.zeros(x.shape, F32)
    for k, o_ref in enumerate((oa_ref, ob_ref, oc_ref, od_ref)):
        cols = slice(k * D_MODEL, (k + 1) * D_MODEL)
        gate = _sigmoid(_dot(u, wgl_ref[:, cols]) + bg_ref[:, cols])
        s = s + gate * _dot(o_ref[...], wbr_ref[k])
    y_ref[...] = x + _dot(s.astype(BF16), wout_ref[...])


def _merge(x, outs, lw, *, tm):
    n = x.shape[0]
    row = lambda w: pl.BlockSpec((tm, w), lambda i: (i, 0))
    return pl.pallas_call(
        _merge_body,
        grid=(n // tm,),
        in_specs=[row(D_MODEL), row(256), row(256), row(256), row(256), _const_spec((1, D_MODEL)),
                  _const_spec((D_MODEL, N_BRANCH * D_MODEL)), _const_spec((1, N_BRANCH * D_MODEL)),
                  _const_spec((N_BRANCH, BRANCH_W, D_MODEL)), _const_spec((D_MODEL, D_MODEL))],
        out_specs=row(D_MODEL),
        out_shape=jax.ShapeDtypeStruct((n, D_MODEL), F32),
        compiler_params=_cparams(("parallel",)), name="merge",
    )(x, *outs, lw["g1"], lw["w_gl"], lw["b_gate"], lw["w_branch"], lw["w_out"])


def _mlp_body(x_ref, g2_ref, w1_ref, w2_ref, gf_ref, y_ref, *, final, chunk):
    x = x_ref[...]
    n = _rms(x, g2_ref[...], D_MODEL).astype(BF16)
    acc = x
    for c in range(D_FF // chunk):
        cols = slice(c * chunk, (c + 1) * chunk)
        h = jnp.maximum(_dot(n, w1_ref[:, cols]), 0.0)
        acc = acc + _dot((h * h).astype(BF16), w2_ref[cols, :])
    if final:
        acc = _rms(acc, gf_ref[...], D_MODEL)
    y_ref[...] = acc


def _mlp(x, lw, gf, *, tm, final):
    n = x.shape[0]
    row = pl.BlockSpec((tm, D_MODEL), lambda i: (i, 0))
    return pl.pallas_call(
        functools.partial(_mlp_body, final=final, chunk=1024),
        grid=(n // tm,),
        in_specs=[row, _const_spec((1, D_MODEL)), _const_spec((D_MODEL, D_FF)), _const_spec((D_FF, D_MODEL)),
                  _const_spec((1, D_MODEL))],
        out_specs=row,
        out_shape=jax.ShapeDtypeStruct((n, D_MODEL), F32),
        compiler_params=_cparams(("parallel",)), name="mlp",
    )(x, lw["g2"], lw["w_mlp1"], lw["w_mlp2"], gf)


def _page_copy(cache, layer, pt_ref, s, p, n_pages, buf, slot, sem, keys_on_lanes):
    page = pt_ref[s * n_pages + p]
    if keys_on_lanes:
        dst = buf.at[slot, :, pl.ds(pl.multiple_of(p * PAGE, PAGE), PAGE)]
    else:
        dst = buf.at[slot, pl.ds(pl.multiple_of(p * PAGE, PAGE), PAGE)]
    return pltpu.make_async_copy(cache.at[layer, page], dst, sem.at[slot])


def _paged_prefetch(caches, bufs, sems, on_lanes, pt_ref, layer, n_pages):
    s = pl.program_id(0)
    ns = pl.num_programs(0)
    streams = tuple(zip(caches, bufs, sems, on_lanes))

    def issue(sample, slot):
        def body(p, _):
            for cache, buf, sem, kl in streams:
                _page_copy(cache, layer, pt_ref, sample, p, n_pages, buf, slot, sem, kl).start()
            return 0
        lax.fori_loop(0, n_pages, body, 0)

    @pl.when(s == 0)
    def _():
        issue(0, 0)

    @pl.when(s + 1 < ns)
    def _():
        issue(s + 1, (s + 1) % 2)

    slot = s % 2

    def wait(p, _):
        for cache, buf, sem, kl in streams:
            _page_copy(cache, layer, pt_ref, s, p, n_pages, buf, slot, sem, kl).wait()
        return 0
    lax.fori_loop(0, n_pages, wait, 0)
    return slot


def _to_blocks(x, nblk):
    return jnp.concatenate([x[:, b * LANES:(b + 1) * LANES] for b in range(nblk)], axis=0)


def _from_blocks(x, nblk):
    r = x.shape[0] // nblk
    return jnp.concatenate([x[b * r:(b + 1) * r] for b in range(nblk)], axis=1)


def _sb_sample_body(pt_ref, q_ref, suffix_ref, later_blocks_ref, kc_hbm, vc_hbm, o_ref, kbuf, vbuf, sems,
                    *, layer, n_pages):
    slot = _paged_prefetch((kc_hbm, vc_hbm), (kbuf, vbuf), (sems.at[0], sems.at[1]), (True, True), pt_ref, layer,
                           n_pages)
    q = q_ref[...]
    z = _dot(q, kbuf[slot].astype(BF16))[:8]
    zb = _to_blocks(z, n_pages)
    sp = _softplus_neg_abs(zb)
    log_keep = jnp.minimum(-zb, 0.0) - sp
    log_beta = jnp.minimum(zb, 0.0) - sp
    hi, lo = _split2(log_keep)
    suffix = suffix_ref[...]
    later = _dot(hi, suffix) + _dot(lo, suffix)
    total = jnp.broadcast_to(later[:, :1] + log_keep[:, :1], later.shape)
    t_hi, t_mid, t_lo = _split3(total)
    lb = later_blocks_ref[...]
    carry = _dot(lb, t_hi) + _dot(lb, t_mid) + _dot(lb, t_lo)
    w = _from_blocks(jnp.exp(log_beta + later + carry), n_pages)
    w16 = jnp.concatenate([w, jnp.zeros_like(w)], axis=0).astype(BF16)
    o_ref[...] = _dot_nt(w16, vbuf[slot].astype(BF16))


def _sb_sample(q16, cache_k, cache_v, pt_flat, consts, *, layer, n_pages):
    s = q16.shape[0]
    t = n_pages * PAGE
    grid_spec = pltpu.PrefetchScalarGridSpec(
        num_scalar_prefetch=1, grid=(s,),
        in_specs=[pl.BlockSpec((None, ROWS16, LANES), lambda i, pt: (i, 0, 0)),
                  pl.BlockSpec((LANES, LANES), lambda i, pt: (0, 0)),
                  pl.BlockSpec((8 * n_pages, 8 * n_pages), lambda i, pt: (0, 0)),
                  pl.BlockSpec(memory_space=pl.ANY), pl.BlockSpec(memory_space=pl.ANY)],
        out_specs=pl.BlockSpec((None, ROWS16, LANES), lambda i, pt: (i, 0, 0)),
        scratch_shapes=[pltpu.VMEM((2, LANES, t), F32), pltpu.VMEM((2, LANES, t), F32),
                        pltpu.SemaphoreType.DMA((2, 2))])
    return pl.pallas_call(
        functools.partial(_sb_sample_body, layer=layer, n_pages=n_pages),
        grid_spec=grid_spec,
        out_shape=jax.ShapeDtypeStruct((s, ROWS16, LANES), F32),
        compiler_params=_cparams(("arbitrary",)), name="sb_sample",
    )(pt_flat, q16, consts["suffix"], consts["later_blocks"], cache_k, cache_v)


def _moba_sample_body(pt_ref, q_ref, ql_ref, knew_ref, vnew_ref, avg_ref, expand_ref, kc_hbm, vc_hbm, o_ref,
                      kbuf, vbuf, sems, *, layer, n_pages):
    slot = _paged_prefetch((kc_hbm, vc_hbm), (kbuf, vbuf), (sems.at[0], sems.at[1]), (True, True), pt_ref, layer,
                           n_pages)
    t = n_pages * PAGE
    nb = t // MB_BLOCK
    q = q_ref[...]
    k_hi, k_lo = _split2(kbuf[slot])
    avg = avg_ref[...]
    kmean = _dot(k_hi, avg) + _dot(k_lo, avg)
    km_hi, km_lo = _split2(kmean)
    gate = (_dot(q, km_hi) + _dot(ql_ref[...], km_hi) + _dot(q, km_lo)) * (HEAD_DIM ** 0.5)
    blk = lax.broadcasted_iota(jnp.int32, gate.shape, 1)
    sel = _top3_mask(gate, blk < nb)
    chosen = _dot(jnp.where(sel, 1.0, 0.0).astype(BF16), expand_ref[...]) > 0.5
    head = lax.broadcasted_iota(jnp.int32, (ROWS16, 1), 0)
    dist = t - lax.broadcasted_iota(jnp.int32, (ROWS16, t), 1)
    s = _dot(q, k_hi) - _alibi_slope(head) * dist.astype(F32)
    s = jnp.where(chosen, s, NEG_BIG)
    s_new = jnp.sum(q.astype(F32) * knew_ref[...], axis=1, keepdims=True)
    m = jnp.maximum(jnp.max(s, axis=1, keepdims=True), s_new)
    p = jnp.exp(s - m)
    p_new = jnp.exp(s_new - m)
    l = jnp.sum(p, axis=1, keepdims=True) + p_new
    o = _dot_nt(p.astype(BF16), vbuf[slot].astype(BF16)) + p_new * vnew_ref[...]
    o_ref[...] = o / l


def _moba_sample(q16, q16_lo, k_new, v_new, cache_k, cache_v, pt_flat, consts, *, layer, n_pages):
    s = q16.shape[0]
    t = n_pages * PAGE
    nb = t // MB_BLOCK
    grid_spec = pltpu.PrefetchScalarGridSpec(
        num_scalar_prefetch=1, grid=(s,),
        in_specs=[pl.BlockSpec((None, ROWS16, LANES), lambda i, pt: (i, 0, 0)),
                  pl.BlockSpec((None, ROWS16, LANES), lambda i, pt: (i, 0, 0)),
                  pl.BlockSpec((None, 1, LANES), lambda i, pt: (i, 0, 0)),
                  pl.BlockSpec((None, 1, LANES), lambda i, pt: (i, 0, 0)),
                  pl.BlockSpec((t, LANES), lambda i, pt: (0, 0)),
                  pl.BlockSpec((LANES, t), lambda i, pt: (0, 0)),
                  pl.BlockSpec(memory_space=pl.ANY), pl.BlockSpec(memory_space=pl.ANY)],
        out_specs=pl.BlockSpec((None, ROWS16, LANES), lambda i, pt: (i, 0, 0)),
        scratch_shapes=[pltpu.VMEM((2, LANES, t), F32), pltpu.VMEM((2, LANES, t), F32),
                        pltpu.SemaphoreType.DMA((2, 2))])
    assert nb <= LANES
    return pl.pallas_call(
        functools.partial(_moba_sample_body, layer=layer, n_pages=n_pages),
        grid_spec=grid_spec,
        out_shape=jax.ShapeDtypeStruct((s, ROWS16, LANES), F32),
        compiler_params=_cparams(("arbitrary",)), name="moba_sample",
    )(pt_flat, q16, q16_lo, k_new, v_new, consts["avg"], consts["expand"], cache_k, cache_v)


def _mla_sample_body(pt_ref, ql_ref, qp_ref, cnew_ref, pnew_ref, wuv_ref, ckv_hbm, kpe_hbm, o_ref,
                     cbuf, pbuf, sems, *, layer, n_pages):
    slot = _paged_prefetch((ckv_hbm, kpe_hbm), (cbuf, pbuf), (sems.at[0], sems.at[1]), (False, True), pt_ref,
                           layer, n_pages)
    ql = ql_ref[...]
    qp = qp_ref[...]
    cb = cbuf[slot].astype(BF16)
    s = _dot_nt(ql, cb) + _dot(qp, pbuf[slot].astype(BF16))
    s_new = (jnp.sum(ql.astype(F32) * cnew_ref[...], axis=1, keepdims=True)
             + jnp.sum(qp.astype(F32) * pnew_ref[...], axis=1, keepdims=True))
    m = jnp.maximum(jnp.max(s, axis=1, keepdims=True), s_new)
    p = jnp.exp(s - m)
    p_new = jnp.exp(s_new - m)
    l = jnp.sum(p, axis=1, keepdims=True) + p_new
    o_lat = (_dot(p.astype(BF16), cb) + p_new * cnew_ref[...]) / l
    o_ref[...] = _dot(o_lat.astype(BF16), wuv_ref[...])


def _mla_sample(ql16, qp16, ckv_new, kpe_new, wuv_flat, cache_ckv, cache_kpe, pt_flat, *, layer, n_pages):
    s = ql16.shape[0]
    t = n_pages * PAGE
    grid_spec = pltpu.PrefetchScalarGridSpec(
        num_scalar_prefetch=1, grid=(s,),
        in_specs=[pl.BlockSpec((None, ROWS16, LANES), lambda i, pt: (i, 0, 0)),
                  pl.BlockSpec((None, ROWS16, MLA_ROPE), lambda i, pt: (i, 0, 0)),
                  pl.BlockSpec((None, 1, LANES), lambda i, pt: (i, 0, 0)),
                  pl.BlockSpec((None, 1, MLA_ROPE), lambda i, pt: (i, 0, 0)),
                  pl.BlockSpec((MLA_KV_RANK, N_HEADS * MLA_V), lambda i, pt: (0, 0)),
                  pl.BlockSpec(memory_space=pl.ANY), pl.BlockSpec(memory_space=pl.ANY)],
        out_specs=pl.BlockSpec((None, ROWS16, N_HEADS * MLA_V), lambda i, pt: (i, 0, 0)),
        scratch_shapes=[pltpu.VMEM((2, t, LANES), F32), pltpu.VMEM((2, MLA_ROPE, t), F32),
                        pltpu.SemaphoreType.DMA((2, 2))])
    return pl.pallas_call(
        functools.partial(_mla_sample_body, layer=layer, n_pages=n_pages),
        grid_spec=grid_spec,
        out_shape=jax.ShapeDtypeStruct((s, ROWS16, N_HEADS * MLA_V), F32),
        compiler_params=_cparams(("arbitrary",)), name="mla_sample",
    )(pt_flat, ql16, qp16, ckv_new, kpe_new, wuv_flat, cache_ckv, cache_kpe)


def _rope_tables(pos):
    half = MLA_ROPE // 2
    inv = jnp.power(ROPE_THETA, -jnp.arange(half, dtype=F32) / half)
    ang = pos.astype(F32)[:, None] * inv
    cos, sin = jnp.cos(ang), jnp.sin(ang)
    cos32 = jnp.concatenate([cos, cos], axis=1)
    sin32 = jnp.concatenate([-sin, sin], axis=1)
    reps = LANES // MLA_ROPE
    return jnp.tile(cos32, (1, reps)), jnp.tile(sin32, (1, reps))


def _block_diag(blocks):
    n, r, c = blocks.shape
    eye = jnp.eye(n, dtype=blocks.dtype)
    return (eye[:, None, :, None] * blocks[:, :, None, :]).reshape(n * r, n * c)


def _layer_weights(l, norm1_g, w_in, b_gate, w_branch, w_out, q_norm_g, kv_norm_g, w_uq, w_uk, w_uv,
                   conv_w, conv_b, w_rg_a, b_rg_a, w_rg_x, b_rg_x, lru_lambda, norm2_g, w_mlp1, w_mlp2):
    w = w_in[l]
    cq_pad = 256 - MLA_Q_RANK
    w_proj = jnp.concatenate([
        w[:, 0:1024],
        jnp.pad(w[:, 1024:1216], ((0, 0), (0, cq_pad))),
        w[:, 1216:1344],
        jnp.tile(w[:, 1344:1376], (1, LANES // MLA_ROPE)),
        w[:, 1376:1888]], axis=1).astype(BF16)
    w_moba = w[:, 512:896]
    w_proj_lo = (w_moba - w_moba.astype(BF16).astype(F32)).astype(BF16)
    uq = jnp.pad(w_uq[l], ((0, cq_pad), (0, 0), (0, 0)))
    uk = jnp.transpose(w_uk[l], (1, 2, 0))
    return {
        "g1": norm1_g[l][None, :], "w_proj": w_proj, "w_proj_lo": w_proj_lo,
        "qg": jnp.pad(q_norm_g[l], (0, cq_pad))[None, :], "kvg": kv_norm_g[l][None, :],
        "wuq_nope": uq[:, :, :MLA_NOPE].reshape(256, N_HEADS * MLA_NOPE).astype(BF16),
        "wuq_pe": uq[:, :, MLA_NOPE:].reshape(256, N_HEADS * MLA_ROPE).astype(BF16),
        "wuk_bd": _block_diag(uk).astype(BF16),
        "wuv_heads": jnp.transpose(w_uv[l], (1, 0, 2)).astype(BF16),
        "wuv_flat": w_uv[l].reshape(MLA_KV_RANK, N_HEADS * MLA_V).astype(BF16),
        "conv_w": conv_w[l], "conv_b": conv_b[l][None, :],
        "w_rg_a": _block_diag(w_rg_a[l]).astype(BF16), "b_rg_a": b_rg_a[l][None, :],
        "w_rg_x": _block_diag(w_rg_x[l]).astype(BF16), "b_rg_x": b_rg_x[l][None, :],
        "lam": lru_lambda[l][None, :],
        "w_gl": w[:, 1888:].astype(BF16), "b_gate": b_gate[l].reshape(1, N_BRANCH * D_MODEL),
        "w_branch": w_branch[l].astype(BF16), "w_out": w_out[l].astype(BF16),
        "g2": norm2_g[l][None, :], "w_mlp1": w_mlp1[l].astype(BF16), "w_mlp2": w_mlp2[l].astype(BF16),
    }


def _decode_consts(n_pages):
    t = n_pages * PAGE
    j = jnp.arange(LANES)
    suffix = (j[:, None] > j[None, :]).astype(BF16)
    r = jnp.arange(8 * n_pages)
    later_blocks = ((r[None, :] % 8 == r[:, None] % 8) & (r[None, :] > r[:, None])).astype(BF16)
    expand = (jnp.arange(t)[None, :] // MB_BLOCK == jnp.arange(LANES)[:, None]).astype(BF16)
    avg = (expand.T * (1.0 / MB_BLOCK)).astype(BF16)
    return {"suffix": suffix, "later_blocks": later_blocks, "expand": expand, "avg": avg}


def _keys_on_lanes(cache):
    d, n = cache.shape[:2]
    moved = jnp.moveaxis(cache, 2, -1)
    return moved.reshape(d, n, -1, PAGE)


def _rows16(x):
    return jnp.pad(x, ((0, 0), (0, ROWS16 - x.shape[1]), (0, 0)))


def _pick_head_lanes(o16):
    parts = [o16[:, j, (j // 2) * HEAD_DIM:(j // 2 + 1) * HEAD_DIM] for j in range(N_HEADS)]
    return jnp.concatenate(parts, axis=1)


def kernel(x_prompt, x_sample, cache_sb_k, cache_sb_v, cache_moba_k, cache_moba_v, cache_mla_ckv, cache_mla_kpe,
           state_lru_h, state_lru_conv, page_table, norm1_g, w_in, b_gate, w_branch, w_out, q_norm_g, kv_norm_g,
           w_uq, w_uk, w_uv, conv_w, conv_b, w_rg_a, b_rg_a, w_rg_x, b_rg_x, lru_lambda, norm2_g, w_mlp1, w_mlp2,
           final_norm_g):
    batch, seq, _ = x_prompt.shape
    n_dec = x_sample.shape[0]
    depth = w_in.shape[0]
    n_pages = page_table.shape[1]
    past = n_pages * PAGE
    n_pool = cache_sb_k.shape[1]
    assert x_sample.shape[1] == 1 and seq % 512 == 0 and past % MB_BLOCK == 0

    tm_p = 512
    tm_s = n_dec
    nb = seq // MB_BLOCK
    nb_pad = max(LANES, nb)

    tab_p = _rope_tables(jnp.arange(seq, dtype=jnp.int32))
    tab_s = _rope_tables(jnp.full((n_dec,), past, jnp.int32))
    consts = _decode_consts(n_pages)
    pt_flat = page_table.reshape(-1)
    gf = final_norm_g[None, :]

    c_sb_k, c_sb_v = _keys_on_lanes(cache_sb_k), _keys_on_lanes(cache_sb_v)
    c_mb_k, c_mb_v = _keys_on_lanes(cache_moba_k), _keys_on_lanes(cache_moba_v)
    c_kpe = _keys_on_lanes(cache_mla_kpe)

    xp = x_prompt.reshape(batch * seq, D_MODEL)
    xs = x_sample.reshape(n_dec, D_MODEL)
    p_states = [[] for _ in range(8)]
    s_states = [[] for _ in range(8)]

    for l in range(depth):
        lw = _layer_weights(l, norm1_g, w_in, b_gate, w_branch, w_out, q_norm_g, kv_norm_g, w_uq, w_uk, w_uv,
                            conv_w, conv_b, w_rg_a, b_rg_a, w_rg_x, b_rg_x, lru_lambda, norm2_g, w_mlp1, w_mlp2)
        final = l == depth - 1

        (qa, qb, qbl, ka, va, kb, vb, kab, vab, kbb, vbb, qc, _, kc, ckv, kpe, xrg, kmean) = _proj(
            xp, tab_p, lw, tm=tm_p, with_kmean=True)
        kmean = jnp.pad(kmean.reshape(batch, nb, LANES), ((0, 0), (0, nb_pad - nb), (0, 0)))
        o_a = _sb_prompt(qa, kab, vab, batch=batch, seq=seq, tq=256, tk=128)
        o_b = _moba_prompt(qb, qbl, kbb, vbb, kmean, batch=batch, seq=seq)
        o_c = _mla_prompt(qc, kc, lw["wuv_heads"], batch=batch, seq=seq, tq=128, tk=256)
        o_d, h_last = _lru_prompt(xrg, lw, batch=batch, seq=seq, tt=512)
        xp = _mlp(_merge(xp, (o_a, o_b, o_c, o_d), lw, tm=tm_p), lw, gf, tm=tm_p, final=final)
        new_p = (ka.reshape(batch, seq, N_KV, HEAD_DIM), va.reshape(batch, seq, N_KV, HEAD_DIM),
                 kb.reshape(batch, seq, N_KV, HEAD_DIM), vb.reshape(batch, seq, N_KV, HEAD_DIM),
                 ckv.reshape(batch, seq, MLA_KV_RANK), kpe.reshape(batch, seq, MLA_ROPE),
                 h_last.reshape(batch, LRU_WIDTH),
                 xrg.reshape(batch, seq, 512)[:, seq - (CONV_W - 1):, :LRU_WIDTH])
        for lst, a in zip(p_states, new_p):
            lst.append(a)

        (qa, qb, qbl, ka, va, kb, vb, _, _, _, _, qc, qpe, _, ckv, kpe, xrg) = _proj(
            xs, tab_s, lw, tm=tm_s, with_kmean=False)
        qa16 = _rows16(qa.reshape(n_dec, N_HEADS, LANES))
        qb16 = _rows16(qb.reshape(n_dec, N_HEADS, LANES))
        qbl16 = _rows16(qbl.reshape(n_dec, N_HEADS, LANES))
        ql16 = _rows16(qc.reshape(n_dec, N_HEADS, 256)[:, :, :LANES])
        qp16 = _rows16(qpe.reshape(n_dec, N_HEADS, MLA_ROPE))
        o_a = _pick_head_lanes(_sb_sample(qa16, c_sb_k, c_sb_v, pt_flat, consts, layer=l, n_pages=n_pages))
        o_b = _pick_head_lanes(_moba_sample(qb16, qbl16, kb[:, None, :], vb[:, None, :], c_mb_k, c_mb_v, pt_flat,
                                            consts, layer=l, n_pages=n_pages))
        o_c16 = _mla_sample(ql16, qp16, ckv[:, None, :], kpe[:, None, :], lw["wuv_flat"], cache_mla_ckv,
                            c_kpe, pt_flat, layer=l, n_pages=n_pages)
        o_c = jnp.concatenate([o_c16[:, h, h * MLA_V:(h + 1) * MLA_V] for h in range(N_HEADS)], axis=1)
        buf0 = state_lru_conv[l]
        o_d, h_new = _lru_sample(xrg, jnp.transpose(buf0, (1, 0, 2)), state_lru_h[l], lw)
        outs = (o_a.astype(BF16), o_b.astype(BF16), o_c.astype(BF16), o_d)
        xs = _mlp(_merge(xs, outs, lw, tm=tm_s), lw, gf, tm=tm_s, final=final)
        new_s = (ka.reshape(n_dec, 1, N_KV, HEAD_DIM), va.reshape(n_dec, 1, N_KV, HEAD_DIM),
                 kb.reshape(n_dec, 1, N_KV, HEAD_DIM), vb.reshape(n_dec, 1, N_KV, HEAD_DIM),
                 ckv.reshape(n_dec, 1, MLA_KV_RANK), kpe.reshape(n_dec, 1, MLA_ROPE), h_new,
                 jnp.concatenate([buf0[:, 1:], xrg[:, None, :LRU_WIDTH]], axis=1))
        for lst, a in zip(s_states, new_s):
            lst.append(a)

    y_prompt = xp.reshape(batch, seq, D_MODEL)
    y_sample = xs.reshape(n_dec, 1, D_MODEL)
    return (y_prompt, y_sample, *[jnp.stack(s) for s in p_states], *[jnp.stack(s) for s in s_states])
```

```python
import functools

import jax
import jax.numpy as jnp
from jax import lax
from jax.experimental import pallas as pl
from jax.experimental.pallas import tpu as pltpu

F32 = jnp.float32
BF16 = jnp.bfloat16

D_MODEL = 1024
HEAD_DIM = 64
N_HEADS = 4
N_KV = 2
MB_BLOCK = 256
MB_TOPK = 3
MLA_Q_RANK = 192
MLA_KV_RANK = 128
MLA_NOPE = 64
MLA_ROPE = 32
MLA_V = 64
ROPE_THETA = 10000.0
LRU_WIDTH = 256
LRU_BLOCKS = 4
CONV_W = 4
LRU_C = 8.0
N_BRANCH = 4
BRANCH_W = 256
D_FF = 4 * D_MODEL
EPS = 1e-6
PAGE = 128

LANES = 128
ROWS16 = 16
NEG_BIG = -1e30
SB_DEAD = -104.0
SB_CHUNK_PAGES = 4
VMEM_LIMIT = 56 * 1024 * 1024

NT = (((1,), (1,)), ((), ()))


def _dot(a, b):
    return jnp.dot(a, b, preferred_element_type=F32)


def _dot_nt(a, b):
    return lax.dot_general(a, b, NT, preferred_element_type=F32)


def _split2(x):
    hi = x.astype(BF16)
    lo = (x - hi.astype(F32)).astype(BF16)
    return hi, lo


def _split3(x):
    hi = x.astype(BF16)
    r = x - hi.astype(F32)
    mid = r.astype(BF16)
    lo = (r - mid.astype(F32)).astype(BF16)
    return hi, mid, lo


def _rms(x, g, n):
    ms = jnp.sum(x * x, axis=-1, keepdims=True) * (1.0 / n)
    return x * lax.rsqrt(ms + EPS) * g


def _softplus_neg_abs(z):
    return jnp.log(1.0 + jnp.exp(-jnp.abs(z)))


def _gelu(x):
    return 0.5 * x * (1.0 + jnp.tanh(0.7978845608028654 * (x + 0.044715 * x * x * x)))


def _sigmoid(x):
    return 1.0 / (1.0 + jnp.exp(-x))


def _cparams(sem):
    return pltpu.CompilerParams(dimension_semantics=sem, vmem_limit_bytes=VMEM_LIMIT)


def _const_spec(shape):
    nd = len(shape)
    return pl.BlockSpec(shape, lambda *_: (0,) * nd)


def _proj_body(x_ref, g1_ref, w_ref, wlo_ref, qg_ref, kvg_ref, wuqn_ref, wuqp_ref, wuk_ref, cos_ref, sin_ref,
               qa_ref, qb_ref, qbl_ref, ka_ref, va_ref, kb_ref, vb_ref, kab_ref, vab_ref, kbb_ref, vbb_ref,
               qc_ref, qpe_ref, kc_ref, ckv_ref, kpe_ref, xrg_ref, *rest, tm, with_kmean):
    x = x_ref[...]
    u32 = _rms(x, g1_ref[...], D_MODEL)
    u, u_lo = _split2(u32)
    acc = _dot(u, w_ref[...])
    fine = _dot(u_lo, w_ref[:, 512:896]) + _dot(u, wlo_ref[...])
    lane = lax.broadcasted_iota(jnp.int32, (tm, LANES), 1)
    lo = lane < HEAD_DIM

    def pad_heads(q):
        v0, v1 = q[:, :LANES], q[:, LANES:]
        r0, r1 = pltpu.roll(v0, HEAD_DIM, 1), pltpu.roll(v1, HEAD_DIM, 1)
        z = jnp.zeros_like(v0)
        heads = [jnp.where(lo, v0, z), jnp.where(lo, r0, z), jnp.where(lo, z, r1), jnp.where(lo, z, v1)]
        return jnp.concatenate(heads, axis=1) * (HEAD_DIM ** -0.5)

    qa_ref[...] = pad_heads(acc[:, 0:256]).astype(BF16)
    ka, va = acc[:, 256:384], acc[:, 384:512]
    qb_hi, qb_lo = _split2(pad_heads(acc[:, 512:768] + fine[:, :256]))
    qb_ref[...] = qb_hi
    qbl_ref[...] = qb_lo
    kb, vb = acc[:, 768:896] + fine[:, 256:], acc[:, 896:1024]
    ka_ref[...] = ka
    va_ref[...] = va
    kb_ref[...] = kb
    vb_ref[...] = vb
    kab_ref[...] = ka.astype(BF16)
    vab_ref[...] = va.astype(BF16)
    kbb_ref[...] = kb.astype(BF16)
    vbb_ref[...] = vb.astype(BF16)
    if with_kmean:
        kmean_ref = rest[0]
        kmean_ref[...] = jnp.sum(kb.reshape(tm // MB_BLOCK, MB_BLOCK, LANES), axis=1) * (1.0 / MB_BLOCK)

    cqn = _rms(acc[:, 1024:1280], qg_ref[...], MLA_Q_RANK).astype(BF16)
    q_nope = _dot(cqn, wuqn_ref[...]).astype(BF16)
    q_pe = _dot(cqn, wuqp_ref[...])
    q_lat = _dot(q_nope, wuk_ref[...])
    cos, sin = cos_ref[...], sin_ref[...]
    first_half = (lane % MLA_ROPE) < (MLA_ROPE // 2)

    def rope(v):
        swapped = jnp.where(first_half, pltpu.roll(v, LANES - MLA_ROPE // 2, 1), pltpu.roll(v, MLA_ROPE // 2, 1))
        return v * cos + swapped * sin

    scale = (MLA_NOPE + MLA_ROPE) ** -0.5
    q_pe = rope(q_pe) * scale
    qpe_ref[...] = q_pe.astype(BF16)
    pieces = []
    for h in range(N_HEADS):
        pieces.append(q_lat[:, h * LANES:(h + 1) * LANES] * scale)
        pieces.append(jnp.where(lane // MLA_ROPE == h, q_pe, 0.0))
    qc_ref[...] = jnp.concatenate(pieces, axis=1).astype(BF16)

    ckvn = _rms(acc[:, 1280:1408], kvg_ref[...], MLA_KV_RANK)
    kpe4 = rope(acc[:, 1408:1536])
    ckv_ref[...] = ckvn
    kpe_ref[...] = kpe4[:, :MLA_ROPE]
    kc_ref[...] = jnp.concatenate([ckvn, kpe4], axis=1).astype(BF16)
    xrg_ref[...] = acc[:, 1536:2048]


def _proj(x, tables, lw, *, tm, with_kmean):
    n = x.shape[0]
    nt = n // tm
    cos_t, sin_t = tables
    npos = cos_t.shape[0] // tm
    row = lambda w: pl.BlockSpec((tm, w), lambda i: (i, 0))
    tab = pl.BlockSpec((tm, LANES), lambda i: (i % npos, 0))
    out_shape = [
        jax.ShapeDtypeStruct((n, 512), BF16), jax.ShapeDtypeStruct((n, 512), BF16),
        jax.ShapeDtypeStruct((n, 512), BF16),
        jax.ShapeDtypeStruct((n, LANES), F32), jax.ShapeDtypeStruct((n, LANES), F32),
        jax.ShapeDtypeStruct((n, LANES), F32), jax.ShapeDtypeStruct((n, LANES), F32),
        jax.ShapeDtypeStruct((n, LANES), BF16), jax.ShapeDtypeStruct((n, LANES), BF16),
        jax.ShapeDtypeStruct((n, LANES), BF16), jax.ShapeDtypeStruct((n, LANES), BF16),
        jax.ShapeDtypeStruct((n, 1024), BF16), jax.ShapeDtypeStruct((n, LANES), BF16),
        jax.ShapeDtypeStruct((n, 256), BF16), jax.ShapeDtypeStruct((n, LANES), F32),
        jax.ShapeDtypeStruct((n, MLA_ROPE), F32), jax.ShapeDtypeStruct((n, 512), F32),
    ]
    out_specs = [row(512), row(512), row(512), row(LANES), row(LANES), row(LANES), row(LANES),
                 row(LANES), row(LANES), row(LANES), row(LANES),
                 row(1024), row(LANES), row(256), row(LANES), row(MLA_ROPE), row(512)]
    if with_kmean:
        nb = tm // MB_BLOCK
        out_shape.append(jax.ShapeDtypeStruct((nt, nb, LANES), F32))
        out_specs.append(pl.BlockSpec((None, nb, LANES), lambda i: (i, 0, 0)))
    return pl.pallas_call(
        functools.partial(_proj_body, tm=tm, with_kmean=with_kmean),
        grid=(nt,),
        in_specs=[row(D_MODEL), _const_spec((1, D_MODEL)), _const_spec((D_MODEL, 2048)),
                  _const_spec((D_MODEL, 384)),
                  _const_spec((1, 256)), _const_spec((1, LANES)), _const_spec((256, 256)),
                  _const_spec((256, LANES)), _const_spec((256, 512)), tab, tab],
        out_specs=out_specs, out_shape=out_shape,
        compiler_params=_cparams(("parallel",)), name="proj",
    )(x, lw["g1"], lw["w_proj"], lw["w_proj_lo"], lw["qg"], lw["kvg"], lw["wuq_nope"], lw["wuq_pe"], lw["wuk_bd"],
      cos_t, sin_t)


def _head_lanes_out(acc, h, tq):
    a0, a1 = acc[:tq], acc[tq:]
    r0, r1 = pltpu.roll(a0, HEAD_DIM, 1), pltpu.roll(a1, HEAD_DIM, 1)
    lo = lax.broadcasted_iota(jnp.int32, (tq, LANES), 1) < HEAD_DIM
    first = jnp.where(h == 0, a0, r0)
    second = jnp.where(h == 0, r1, a1)
    return jnp.where(lo, first, second)


def _sb_prompt_body(q_ref, k_ref, v_ref, o_ref, *, tq, tk):
    h = pl.program_id(1)
    qi = pl.program_id(2)
    qblk = q_ref[...]
    q = jnp.concatenate([qblk[:, :LANES], qblk[:, LANES:]], axis=0)
    rows = 2 * tq
    t_pos = qi * tq + lax.broadcasted_iota(jnp.int32, (rows, tk), 0) % tq
    s_loc = lax.broadcasted_iota(jnp.int32, (rows, tk), 1)
    suffix = (lax.broadcasted_iota(jnp.int32, (tk, tk), 0) > lax.broadcasted_iota(jnp.int32, (tk, tk), 1)
              ).astype(BF16)

    def block(kb, carry, acc, masked):
        start = pl.multiple_of(kb * tk, tk)
        z = _dot_nt(q, k_ref[pl.ds(start, tk), :])
        sp = _softplus_neg_abs(z)
        log_keep = jnp.minimum(-z, 0.0) - sp
        log_beta = jnp.minimum(z, 0.0) - sp
        if masked:
            causal = (start + s_loc) < t_pos
            log_keep = jnp.where(causal, log_keep, 0.0)
        hi, lo = _split2(log_keep)
        later = _dot(hi, suffix) + _dot(lo, suffix)
        w = jnp.exp(log_beta + later + carry)
        if masked:
            w = jnp.where(causal, w, 0.0)
        acc = acc + _dot(w.astype(BF16), v_ref[pl.ds(start, tk), :])
        carry = carry + later[:, :1] + log_keep[:, :1]
        return carry, acc

    carry = jnp.zeros((rows, 1), F32)
    acc = jnp.zeros((rows, LANES), F32)
    n_diag = tq // tk
    for d in range(n_diag):
        carry, acc = block(qi * n_diag + (n_diag - 1 - d), carry, acc, True)

    def cond(c):
        kb, carry, _ = c
        return jnp.logical_and(kb >= 0, jnp.max(carry) > SB_DEAD)

    def step(c):
        kb, carry, acc = c
        carry, acc = block(kb, carry, acc, False)
        return kb - 1, carry, acc

    _, _, acc = lax.while_loop(cond, step, (qi * n_diag - 1, carry, acc))
    o_ref[...] = _head_lanes_out(acc, h, tq).astype(o_ref.dtype)


def _sb_prompt(qa_pad, kab, vab, *, batch, seq, tq, tk):
    nq = seq // tq
    return pl.pallas_call(
        functools.partial(_sb_prompt_body, tq=tq, tk=tk),
        grid=(batch, N_KV, nq),
        in_specs=[pl.BlockSpec((tq, 256), lambda b, h, i: (b * nq + i, h)),
                  pl.BlockSpec((seq, LANES), lambda b, h, i: (b, 0)),
                  pl.BlockSpec((seq, LANES), lambda b, h, i: (b, 0))],
        out_specs=pl.BlockSpec((tq, LANES), lambda b, h, i: (b * nq + i, h)),
        out_shape=jax.ShapeDtypeStruct((batch * seq, 256), BF16),
        compiler_params=_cparams(("parallel", "parallel", "arbitrary")), name="sb_prompt",
    )(qa_pad, kab, vab)


def _top3(gate, valid):
    nb = gate.shape[1]
    idx = lax.broadcasted_iota(jnp.int32, gate.shape, 1).astype(F32)
    g = jnp.where(valid, gate, -jnp.inf)
    sel = jnp.zeros(gate.shape, jnp.bool_)
    picks, oks = [], []
    for _ in range(MB_TOPK):
        m = jnp.max(g, axis=1, keepdims=True)
        first = jnp.min(jnp.where(g == m, idx, float(nb)), axis=1, keepdims=True)
        pick = idx == first
        sel = jnp.logical_or(sel, pick)
        g = jnp.where(pick, -jnp.inf, g)
        picks.append(first)
        oks.append(m > -jnp.inf)
    return jnp.logical_and(sel, valid), picks, oks


def _top3_mask(gate, valid):
    return _top3(gate, valid)[0]


def _alibi_slope(head):
    return jnp.exp2(-2.0 * (head + 1).astype(F32))


def _moba_prompt_body(q_ref, ql_ref, k_ref, v_ref, km_ref, o_ref, *, tq, nb_pad):
    h = pl.program_id(1)
    qi = pl.program_id(2)
    tk = MB_BLOCK
    groups = N_HEADS // N_KV
    rows = groups * tq
    own = (qi * tq) // tk
    q = jnp.concatenate([q_ref[:, g * LANES:(g + 1) * LANES] for g in range(groups)], axis=0)
    q_lo = jnp.concatenate([ql_ref[:, g * LANES:(g + 1) * LANES] for g in range(groups)], axis=0)
    row = lax.broadcasted_iota(jnp.int32, (rows, 1), 0)
    t_off = qi * tq - own * tk + row % tq
    s_loc = lax.broadcasted_iota(jnp.int32, (rows, tk), 1)
    key_off = lax.broadcasted_iota(jnp.int32, (1, tk), 1).astype(F32)
    blk = lax.broadcasted_iota(jnp.int32, (rows, nb_pad), 1)
    slope = _alibi_slope(2 * h + row // tq)

    km_hi, km_lo = _split2(km_ref[...])
    gate = (_dot_nt(q, km_hi) + _dot_nt(q_lo, km_hi) + _dot_nt(q, km_lo)) * (HEAD_DIM ** 0.5)
    sel = _top3_mask(gate, blk < own).astype(F32)
    row_bias = slope * t_off.astype(F32)

    def scores(n):
        start = pl.multiple_of(n * tk, tk)
        return _dot_nt(q, k_ref[pl.ds(start, tk), :])

    def process(s, n, c, own_block):
        m, l, acc = c
        start = pl.multiple_of(n * tk, tk)
        s = s + slope * key_off
        if own_block:
            s = jnp.where(s_loc <= t_off, s, NEG_BIG)
            shift = row_bias
        else:
            chosen = jnp.sum(jnp.where(blk == n, sel, 0.0), axis=1, keepdims=True) > 0.5
            gap = jnp.full((1, 1), (own - n) * tk, jnp.int32).astype(F32)
            shift = row_bias + slope * gap + jnp.where(chosen, 0.0, -NEG_BIG)
        m_new = jnp.maximum(m, jnp.max(s, axis=1, keepdims=True) - shift)
        alpha = jnp.exp(m - m_new)
        p = jnp.exp(s - (shift + m_new))
        l = alpha * l + jnp.sum(p, axis=1, keepdims=True)
        acc = alpha * acc + _dot(p.astype(BF16), v_ref[pl.ds(start, tk), :])
        return m_new, l, acc

    init = (jnp.full((rows, 1), NEG_BIG, F32), jnp.zeros((rows, 1), F32), jnp.zeros((rows, LANES), F32))
    state = process(scores(own), own, init, True)

    def body(n, c):
        s_cur, state = c
        s_next = scores(jnp.minimum(n + 1, own))
        return s_next, process(s_cur, n, state, False)

    _, (m, l, acc) = lax.fori_loop(0, own, body, (scores(0), state))
    o_ref[...] = _head_lanes_out(acc / l, h, tq).astype(o_ref.dtype)


def _moba_prompt(qb_pad, qb_lo, kbb, vbb, kmean, *, batch, seq, tq):
    nq = seq // tq
    nb_pad = kmean.shape[1]
    return pl.pallas_call(
        functools.partial(_moba_prompt_body, tq=tq, nb_pad=nb_pad),
        grid=(batch, N_KV, nq),
        in_specs=[pl.BlockSpec((tq, 256), lambda b, h, i: (b * nq + i, h)),
                  pl.BlockSpec((tq, 256), lambda b, h, i: (b * nq + i, h)),
                  pl.BlockSpec((seq, LANES), lambda b, h, i: (b, 0)),
                  pl.BlockSpec((seq, LANES), lambda b, h, i: (b, 0)),
                  pl.BlockSpec((None, nb_pad, LANES), lambda b, h, i: (b, 0, 0))],
        out_specs=pl.BlockSpec((tq, LANES), lambda b, h, i: (b * nq + i, h)),
        out_shape=jax.ShapeDtypeStruct((batch * seq, 256), BF16),
        compiler_params=_cparams(("parallel", "parallel", "arbitrary")), name="moba_prompt",
    )(qb_pad, qb_lo, kbb, vbb, kmean)


def _mla_prompt_body(q_ref, kc_ref, wuv_ref, o_ref, *, tq, tk):
    qi = pl.program_id(1)
    q = jnp.concatenate([q_ref[:, h * 256:(h + 1) * 256] for h in range(N_HEADS)], axis=0)
    rows = N_HEADS * tq
    t_loc = lax.broadcasted_iota(jnp.int32, (rows, tk), 0) % tq
    s_loc = lax.broadcasted_iota(jnp.int32, (rows, tk), 1)
    diag = (qi * tq) // tk

    def scores(kb):
        start = pl.multiple_of(kb * tk, tk)
        return _dot_nt(q, kc_ref[pl.ds(start, tk), :])

    def process(s, kb, c, masked):
        m, l, acc = c
        start = pl.multiple_of(kb * tk, tk)
        if masked:
            s = jnp.where(start + s_loc <= qi * tq + t_loc, s, NEG_BIG)
        m_new = jnp.maximum(m, jnp.max(s, axis=1, keepdims=True))
        alpha = jnp.exp(m - m_new)
        p = jnp.exp(s - m_new)
        l = alpha * l + jnp.sum(p, axis=1, keepdims=True)
        acc = alpha * acc + _dot(p.astype(BF16), kc_ref[pl.ds(start, tk), :MLA_KV_RANK])
        return m_new, l, acc

    def body(i, c):
        s_cur, state = c
        s_next = scores(i + 1)
        return s_next, process(s_cur, i, state, False)

    init = (jnp.full((rows, 1), NEG_BIG, F32), jnp.zeros((rows, 1), F32), jnp.zeros((rows, MLA_KV_RANK), F32))
    s_cur, state = lax.fori_loop(0, diag, body, (scores(0), init))
    m, l, acc = process(s_cur, diag, state, True)
    o_lat = (acc / l).astype(BF16)
    outs = [_dot(o_lat[h * tq:(h + 1) * tq], wuv_ref[h]) for h in range(N_HEADS)]
    o_ref[...] = jnp.concatenate(outs, axis=1).astype(o_ref.dtype)


def _mla_prompt(qc, kc, wuv, *, batch, seq, tq, tk):
    nq = seq // tq
    return pl.pallas_call(
        functools.partial(_mla_prompt_body, tq=tq, tk=tk),
        grid=(batch, nq),
        in_specs=[pl.BlockSpec((tq, 1024), lambda b, i: (b * nq + i, 0)),
                  pl.BlockSpec((seq, 256), lambda b, i: (b, 0)),
                  _const_spec((N_HEADS, MLA_KV_RANK, MLA_V))],
        out_specs=pl.BlockSpec((tq, 256), lambda b, i: (b * nq + i, 0)),
        out_shape=jax.ShapeDtypeStruct((batch * seq, 256), BF16),
        compiler_params=_cparams(("parallel", "arbitrary")), name="mla_prompt",
    )(qc, kc, wuv)


def _lru_gates(xc, wa_ref, ba_ref, wx_ref, bx_ref, lam_ref):
    xb = xc.astype(BF16)
    r = _sigmoid(_dot(xb, wa_ref[...]) + ba_ref[...])
    i = _sigmoid(_dot(xb, wx_ref[...]) + bx_ref[...])
    lam = lam_ref[...]
    softplus_neg_lam = jnp.maximum(-lam, 0.0) + _softplus_neg_abs(lam)
    log_a = -LRU_C * r * softplus_neg_lam
    a = jnp.exp(log_a)
    th = jnp.tanh(log_a)
    u = jnp.sqrt(-2.0 * th / (1.0 - th)) * i * xc
    return a, u


def _lru_prompt_body(xrg_ref, cw_ref, cb_ref, wa_ref, ba_ref, wx_ref, bx_ref, lam_ref,
                     od_ref, hlast_ref, xpad, hprev, *, tt):
    t = pl.program_id(1)

    @pl.when(t == 0)
    def _():
        xpad[0:8, :] = jnp.zeros((8, LRU_WIDTH), F32)
        hprev[...] = jnp.zeros((1, LRU_WIDTH), F32)

    x = xrg_ref[:, :LRU_WIDTH]
    xg = xrg_ref[:, LRU_WIDTH:]
    xpad[8:8 + tt, :] = x
    xc = cb_ref[...] + cw_ref[CONV_W - 1:CONV_W, :] * x
    for i in range(CONV_W - 1):
        back = CONV_W - 1 - i
        xc = xc + cw_ref[i:i + 1, :] * xpad[8 - back:8 - back + tt, :]
    xpad[0:8, :] = x[tt - 8:, :]

    a, u = _lru_gates(xc, wa_ref, ba_ref, wx_ref, bx_ref, lam_ref)
    row = lax.broadcasted_iota(jnp.int32, (tt, LRU_WIDTH), 0)
    d = 1
    while d < tt:
        keep = row >= d
        a_sh = jnp.where(keep, pltpu.roll(a, d, 0), 1.0)
        u_sh = jnp.where(keep, pltpu.roll(u, d, 0), 0.0)
        u = u + a * u_sh
        a = a * a_sh
        d *= 2
    hs = a * hprev[...] + u
    hprev[...] = hs[tt - 1:, :]
    hlast_ref[...] = hs[tt - 1:, :]
    od_ref[...] = (hs * _gelu(xg)).astype(od_ref.dtype)


def _lru_prompt(xrg, lw, *, batch, seq, tt):
    nt = seq // tt
    vec = _const_spec((1, LRU_WIDTH))
    mat = _const_spec((LRU_WIDTH, LRU_WIDTH))
    return pl.pallas_call(
        functools.partial(_lru_prompt_body, tt=tt),
        grid=(batch, nt),
        in_specs=[pl.BlockSpec((tt, 512), lambda b, t: (b * nt + t, 0)),
                  _const_spec((CONV_W, LRU_WIDTH)), vec, mat, vec, mat, vec, vec],
        out_specs=[pl.BlockSpec((tt, LRU_WIDTH), lambda b, t: (b * nt + t, 0)),
                   pl.BlockSpec((None, 1, LRU_WIDTH), lambda b, t: (b, 0, 0))],
        out_shape=[jax.ShapeDtypeStruct((batch * seq, LRU_WIDTH), BF16),
                   jax.ShapeDtypeStruct((batch, 1, LRU_WIDTH), F32)],
        scratch_shapes=[pltpu.VMEM((tt + 8, LRU_WIDTH), F32), pltpu.VMEM((1, LRU_WIDTH), F32)],
        compiler_params=_cparams(("parallel", "arbitrary")), name="lru_prompt",
    )(xrg, lw["conv_w"], lw["conv_b"], lw["w_rg_a"], lw["b_rg_a"], lw["w_rg_x"], lw["b_rg_x"], lw["lam"])


def _lru_sample_body(xrg_ref, buf_ref, h0_ref, cw_ref, cb_ref, wa_ref, ba_ref, wx_ref, bx_ref, lam_ref,
                     od_ref, h_ref):
    x = xrg_ref[:, :LRU_WIDTH]
    xg = xrg_ref[:, LRU_WIDTH:]
    xc = cb_ref[...] + cw_ref[CONV_W - 1:CONV_W, :] * x
    for i in range(CONV_W - 1):
        xc = xc + cw_ref[i:i + 1, :] * buf_ref[i]
    a, u = _lru_gates(xc, wa_ref, ba_ref, wx_ref, bx_ref, lam_ref)
    h = a * h0_ref[...] + u
    h_ref[...] = h
    od_ref[...] = (h * _gelu(xg)).astype(od_ref.dtype)


def _lru_sample(xrg, buf_t, h0, lw):
    s = xrg.shape[0]
    vec = _const_spec((1, LRU_WIDTH))
    mat = _const_spec((LRU_WIDTH, LRU_WIDTH))
    return pl.pallas_call(
        _lru_sample_body,
        grid=(1,),
        in_specs=[_const_spec((s, 512)), _const_spec((CONV_W - 1, s, LRU_WIDTH)), _const_spec((s, LRU_WIDTH)),
                  _const_spec((CONV_W, LRU_WIDTH)), vec, mat, vec, mat, vec, vec],
        out_specs=[_const_spec((s, LRU_WIDTH)), _const_spec((s, LRU_WIDTH))],
        out_shape=[jax.ShapeDtypeStruct((s, LRU_WIDTH), BF16), jax.ShapeDtypeStruct((s, LRU_WIDTH), F32)],
        compiler_params=_cparams(("arbitrary",)), name="lru_sample",
    )(xrg, buf_t, h0, lw["conv_w"], lw["conv_b"], lw["w_rg_a"], lw["b_rg_a"], lw["w_rg_x"], lw["b_rg_x"],
      lw["lam"])


def _merge_body(x_ref, oa_ref, ob_ref, oc_ref, od_ref, g1_ref, wgl_ref, bg_ref, wbr_ref, wout_ref, y_ref):
    x = x_ref[...]
    u = _rms(x, g1_ref[...], D_MODEL).astype(BF16)
    s = jnp.zeros(x.shape, F32)
    for k, o_ref in enumerate((oa_ref, ob_ref, oc_ref, od_ref)):
        cols = slice(k * D_MODEL, (k + 1) * D_MODEL)
        gate = _sigmoid(_dot(u, wgl_ref[:, cols]) + bg_ref[:, cols])
        s = s + gate * _dot(o_ref[...], wbr_ref[k])
    y_ref[...] = x + _dot(s.astype(BF16), wout_ref[...])


def _merge(x, outs, lw, *, tm):
    n = x.shape[0]
    row = lambda w: pl.BlockSpec((tm, w), lambda i: (i, 0))
    return pl.pallas_call(
        _merge_body,
        grid=(n // tm,),
        in_specs=[row(D_MODEL), row(256), row(256), row(256), row(256), _const_spec((1, D_MODEL)),
                  _const_spec((D_MODEL, N_BRANCH * D_MODEL)), _const_spec((1, N_BRANCH * D_MODEL)),
                  _const_spec((N_BRANCH, BRANCH_W, D_MODEL)), _const_spec((D_MODEL, D_MODEL))],
        out_specs=row(D_MODEL),
        out_shape=jax.ShapeDtypeStruct((n, D_MODEL), F32),
        compiler_params=_cparams(("parallel",)), name="merge",
    )(x, *outs, lw["g1"], lw["w_gl"], lw["b_gate"], lw["w_branch"], lw["w_out"])


def _mlp_body(x_ref, g2_ref, w1_ref, w2_ref, gf_ref, y_ref, *, final, chunk):
    x = x_ref[...]
    n = _rms(x, g2_ref[...], D_MODEL).astype(BF16)
    acc = x
    for c in range(D_FF // chunk):
        cols = slice(c * chunk, (c + 1) * chunk)
        h = jnp.maximum(_dot(n, w1_ref[:, cols]), 0.0)
        acc = acc + _dot((h * h).astype(BF16), w2_ref[cols, :])
    if final:
        acc = _rms(acc, gf_ref[...], D_MODEL)
    y_ref[...] = acc


def _mlp(x, lw, gf, *, tm, final):
    n = x.shape[0]
    row = pl.BlockSpec((tm, D_MODEL), lambda i: (i, 0))
    return pl.pallas_call(
        functools.partial(_mlp_body, final=final, chunk=1024),
        grid=(n // tm,),
        in_specs=[row, _const_spec((1, D_MODEL)), _const_spec((D_MODEL, D_FF)), _const_spec((D_FF, D_MODEL)),
                  _const_spec((1, D_MODEL))],
        out_specs=row,
        out_shape=jax.ShapeDtypeStruct((n, D_MODEL), F32),
        compiler_params=_cparams(("parallel",)), name="mlp",
    )(x, lw["g2"], lw["w_mlp1"], lw["w_mlp2"], gf)


def _page_copy(cache, layer, pt_ref, s, p, n_pages, buf, slot, sem, layout):
    page = pt_ref[s * n_pages + p]
    if layout == "lanes":
        dst = buf.at[slot, :, pl.ds(pl.multiple_of(p * PAGE, PAGE), PAGE)]
    elif layout == "blocks":
        per = MB_BLOCK // PAGE
        dst = buf.at[slot, p // per, :, pl.ds(pl.multiple_of((p % per) * PAGE, PAGE), PAGE)]
    else:
        dst = buf.at[slot, pl.ds(pl.multiple_of(p * PAGE, PAGE), PAGE)]
    return pltpu.make_async_copy(cache.at[layer, page], dst, sem.at[slot])


def _paged_prefetch(caches, bufs, sems, layouts, pt_ref, layer, n_pages):
    s = pl.program_id(0)
    ns = pl.num_programs(0)
    streams = tuple(zip(caches, bufs, sems, layouts))

    def issue(sample, slot):
        def body(p, _):
            for cache, buf, sem, kl in streams:
                _page_copy(cache, layer, pt_ref, sample, p, n_pages, buf, slot, sem, kl).start()
            return 0
        lax.fori_loop(0, n_pages, body, 0)

    @pl.when(s == 0)
    def _():
        issue(0, 0)

    @pl.when(s + 1 < ns)
    def _():
        issue(s + 1, (s + 1) % 2)

    slot = s % 2

    def wait(p, _):
        for cache, buf, sem, kl in streams:
            _page_copy(cache, layer, pt_ref, s, p, n_pages, buf, slot, sem, kl).wait()
        return 0
    lax.fori_loop(0, n_pages, wait, 0)
    return slot


def _to_blocks(x, nblk):
    return jnp.concatenate([x[:, b * LANES:(b + 1) * LANES] for b in range(nblk)], axis=0)


def _from_blocks(x, nblk):
    r = x.shape[0] // nblk
    return jnp.concatenate([x[b * r:(b + 1) * r] for b in range(nblk)], axis=1)


def _sb_sample_body(pt_ref, q_ref, suffix_ref, kc_hbm, vc_hbm, o_ref, kbuf, vbuf, kmore, vmore, sems, more_sems,
                    *, layer, n_pages, cp):
    s = pl.program_id(0)
    ns = pl.num_programs(0)
    n_chunks = n_pages // cp

    def chunk_copies(sample, j, kdst, vdst, ksem, vsem):
        copies = []
        for i in range(cp):
            page = pt_ref[sample * n_pages + n_pages - (j + 1) * cp + i]
            lanes = pl.ds(i * PAGE, PAGE)
            copies.append(pltpu.make_async_copy(kc_hbm.at[layer, page], kdst.at[:, lanes], ksem))
            copies.append(pltpu.make_async_copy(vc_hbm.at[layer, page], vdst.at[:, lanes], vsem))
        return copies

    def first_chunk(sample, slot):
        return chunk_copies(sample, 0, kbuf.at[slot], vbuf.at[slot], sems.at[0, slot], sems.at[1, slot])

    @pl.when(s == 0)
    def _():
        for c in first_chunk(0, 0):
            c.start()

    @pl.when(s + 1 < ns)
    def _():
        for c in first_chunk(s + 1, (s + 1) % 2):
            c.start()

    slot = s % 2
    for c in first_chunk(s, slot):
        c.wait()

    q = q_ref[...]
    suffix = suffix_ref[...]
    r = lax.broadcasted_iota(jnp.int32, (8 * cp, 8 * cp), 0)
    c_ = lax.broadcasted_iota(jnp.int32, (8 * cp, 8 * cp), 1)
    later_pages = jnp.logical_and(c_ % 8 == r % 8, c_ > r).astype(BF16)

    def process(k_t, v_t, carry, acc):
        z = _dot(q, k_t.astype(BF16))[:8]
        zb = _to_blocks(z, cp)
        sp = _softplus_neg_abs(zb)
        log_keep = jnp.minimum(-zb, 0.0) - sp
        log_beta = jnp.minimum(zb, 0.0) - sp
        hi, lo = _split2(log_keep)
        later = _dot(hi, suffix) + _dot(lo, suffix)
        total = jnp.broadcast_to(later[:, :1] + log_keep[:, :1], later.shape)
        t_hi, t_mid, t_lo = _split3(total)
        cross = _dot(later_pages, t_hi) + _dot(later_pages, t_mid) + _dot(later_pages, t_lo)
        carry_rows = jnp.concatenate([carry] * cp, axis=0)
        w = _from_blocks(jnp.exp(log_beta + later + cross + carry_rows), cp)
        w16 = jnp.concatenate([w, jnp.zeros_like(w)], axis=0).astype(BF16)
        acc = acc + _dot_nt(w16, v_t.astype(BF16))
        carry = carry + cross[:8, :1] + total[:8, :1]
        return carry, acc

    carry, acc = process(kbuf[slot], vbuf[slot], jnp.zeros((8, 1), F32), jnp.zeros((ROWS16, LANES), F32))

    def cond(c):
        j, carry, _ = c
        return jnp.logical_and(j < n_chunks, jnp.max(carry[:N_HEADS]) > SB_DEAD)

    def body(c):
        j, carry, acc = c
        copies = chunk_copies(s, j, kmore, vmore, more_sems.at[0], more_sems.at[1])
        for cpy in copies:
            cpy.start()
        for cpy in copies:
            cpy.wait()
        carry, acc = process(kmore[...], vmore[...], carry, acc)
        return j + 1, carry, acc

    _, _, acc = lax.while_loop(cond, body, (1, carry, acc))
    o_ref[...] = acc


def _sb_sample(q16, cache_k, cache_v, pt_flat, consts, *, layer, n_pages, cp):
    s = q16.shape[0]
    assert n_pages % cp == 0
    grid_spec = pltpu.PrefetchScalarGridSpec(
        num_scalar_prefetch=1, grid=(s,),
        in_specs=[pl.BlockSpec((None, ROWS16, LANES), lambda i, pt: (i, 0, 0)),
                  pl.BlockSpec((LANES, LANES), lambda i, pt: (0, 0)),
                  pl.BlockSpec(memory_space=pl.ANY), pl.BlockSpec(memory_space=pl.ANY)],
        out_specs=pl.BlockSpec((None, ROWS16, LANES), lambda i, pt: (i, 0, 0)),
        scratch_shapes=[pltpu.VMEM((2, LANES, cp * PAGE), F32), pltpu.VMEM((2, LANES, cp * PAGE), F32),
                        pltpu.VMEM((LANES, cp * PAGE), F32), pltpu.VMEM((LANES, cp * PAGE), F32),
                        pltpu.SemaphoreType.DMA((2, 2)), pltpu.SemaphoreType.DMA((2,))])
    return pl.pallas_call(
        functools.partial(_sb_sample_body, layer=layer, n_pages=n_pages, cp=cp),
        grid_spec=grid_spec,
        out_shape=jax.ShapeDtypeStruct((s, ROWS16, LANES), F32),
        compiler_params=_cparams(("arbitrary",)), name="sb_sample",
    )(pt_flat, q16, consts["suffix"], cache_k, cache_v)


def _moba_sample_body(pt_ref, q_ref, ql_ref, knew_ref, vnew_ref, kc_hbm, vc_hbm, o_ref, kbuf, vbuf, sems,
                      *, layer, n_pages):
    slot = _paged_prefetch((kc_hbm, vc_hbm), (kbuf, vbuf), (sems.at[0], sems.at[1]), ("blocks", "blocks"), pt_ref,
                           layer, n_pages)
    past = n_pages * PAGE
    nb = past // MB_BLOCK
    q = q_ref[...]
    kmean = jnp.sum(kbuf[slot], axis=2) * (1.0 / MB_BLOCK)
    km_hi, km_lo = _split2(kmean)
    gate = (_dot_nt(q, km_hi) + _dot_nt(ql_ref[...], km_hi) + _dot_nt(q, km_lo)) * (HEAD_DIM ** 0.5)
    _, picks, oks = _top3(gate, jnp.ones(gate.shape, jnp.bool_))

    head = lax.broadcasted_iota(jnp.int32, (ROWS16, 1), 0)
    slope = _alibi_slope(head)
    key_off = lax.broadcasted_iota(jnp.int32, (ROWS16, MB_BLOCK), 1)
    chosen = [[jnp.minimum(picks[i][j, 0].astype(jnp.int32), nb - 1) for j in range(N_HEADS)]
              for i in range(MB_TOPK)]

    tiles = []
    for i in range(MB_TOPK):
        tile = jnp.zeros((ROWS16, MB_BLOCK), F32)
        for j in range(N_HEADS):
            n = chosen[i][j]
            dist = (past - n * MB_BLOCK) - key_off
            s_ij = _dot(q, kbuf[slot, n].astype(BF16)) - slope * dist.astype(F32)
            tile = jnp.where(head == j, s_ij, tile)
        tiles.append(jnp.where(oks[i], tile, NEG_BIG))
    s_new = jnp.sum(q.astype(F32) * knew_ref[...], axis=1, keepdims=True)
    m = s_new
    for tile in tiles:
        m = jnp.maximum(m, jnp.max(tile, axis=1, keepdims=True))
    p_new = jnp.exp(s_new - m)
    l = p_new
    o = p_new * vnew_ref[...]
    for i in range(MB_TOPK):
        p = jnp.exp(tiles[i] - m)
        l = l + jnp.sum(p, axis=1, keepdims=True)
        for j in range(N_HEADS):
            p_j = jnp.where(head == j, p, 0.0).astype(BF16)
            o = o + _dot_nt(p_j, vbuf[slot, chosen[i][j]].astype(BF16))
    o_ref[...] = o / l


def _moba_sample(q16, q16_lo, k_new, v_new, cache_k, cache_v, pt_flat, *, layer, n_pages):
    s = q16.shape[0]
    nb = n_pages * PAGE // MB_BLOCK
    assert nb >= MB_TOPK and n_pages * PAGE % MB_BLOCK == 0
    grid_spec = pltpu.PrefetchScalarGridSpec(
        num_scalar_prefetch=1, grid=(s,),
        in_specs=[pl.BlockSpec((None, ROWS16, LANES), lambda i, pt: (i, 0, 0)),
                  pl.BlockSpec((None, ROWS16, LANES), lambda i, pt: (i, 0, 0)),
                  pl.BlockSpec((None, 1, LANES), lambda i, pt: (i, 0, 0)),
                  pl.BlockSpec((None, 1, LANES), lambda i, pt: (i, 0, 0)),
                  pl.BlockSpec(memory_space=pl.ANY), pl.BlockSpec(memory_space=pl.ANY)],
        out_specs=pl.BlockSpec((None, ROWS16, LANES), lambda i, pt: (i, 0, 0)),
        scratch_shapes=[pltpu.VMEM((2, nb, LANES, MB_BLOCK), F32), pltpu.VMEM((2, nb, LANES, MB_BLOCK), F32),
                        pltpu.SemaphoreType.DMA((2, 2))])
    return pl.pallas_call(
        functools.partial(_moba_sample_body, layer=layer, n_pages=n_pages),
        grid_spec=grid_spec,
        out_shape=jax.ShapeDtypeStruct((s, ROWS16, LANES), F32),
        compiler_params=_cparams(("arbitrary",)), name="moba_sample",
    )(pt_flat, q16, q16_lo, k_new, v_new, cache_k, cache_v)


def _mla_sample_body(pt_ref, ql_ref, qp_ref, cnew_ref, pnew_ref, wuv_ref, ckv_hbm, kpe_hbm, o_ref,
                     cbuf, pbuf, sems, *, layer, n_pages):
    slot = _paged_prefetch((ckv_hbm, kpe_hbm), (cbuf, pbuf), (sems.at[0], sems.at[1]), ("rows", "lanes"), pt_ref,
                           layer, n_pages)
    ql = ql_ref[...]
    qp = qp_ref[...]
    cb = cbuf[slot].astype(BF16)
    s = _dot_nt(ql, cb) + _dot(qp, pbuf[slot].astype(BF16))
    s_new = (jnp.sum(ql.astype(F32) * cnew_ref[...], axis=1, keepdims=True)
             + jnp.sum(qp.astype(F32) * pnew_ref[...], axis=1, keepdims=True))
    m = jnp.maximum(jnp.max(s, axis=1, keepdims=True), s_new)
    p = jnp.exp(s - m)
    p_new = jnp.exp(s_new - m)
    l = jnp.sum(p, axis=1, keepdims=True) + p_new
    o_lat = (_dot(p.astype(BF16), cb) + p_new * cnew_ref[...]) / l
    o_ref[...] = _dot(o_lat.astype(BF16), wuv_ref[...])


def _mla_sample(ql16, qp16, ckv_new, kpe_new, wuv_flat, cache_ckv, cache_kpe, pt_flat, *, layer, n_pages):
    s = ql16.shape[0]
    t = n_pages * PAGE
    grid_spec = pltpu.PrefetchScalarGridSpec(
        num_scalar_prefetch=1, grid=(s,),
        in_specs=[pl.BlockSpec((None, ROWS16, LANES), lambda i, pt: (i, 0, 0)),
                  pl.BlockSpec((None, ROWS16, MLA_ROPE), lambda i, pt: (i, 0, 0)),
                  pl.BlockSpec((None, 1, LANES), lambda i, pt: (i, 0, 0)),
                  pl.BlockSpec((None, 1, MLA_ROPE), lambda i, pt: (i, 0, 0)),
                  pl.BlockSpec((MLA_KV_RANK, N_HEADS * MLA_V), lambda i, pt: (0, 0)),
                  pl.BlockSpec(memory_space=pl.ANY), pl.BlockSpec(memory_space=pl.ANY)],
        out_specs=pl.BlockSpec((None, ROWS16, N_HEADS * MLA_V), lambda i, pt: (i, 0, 0)),
        scratch_shapes=[pltpu.VMEM((2, t, LANES), F32), pltpu.VMEM((2, MLA_ROPE, t), F32),
                        pltpu.SemaphoreType.DMA((2, 2))])
    return pl.pallas_call(
        functools.partial(_mla_sample_body, layer=layer, n_pages=n_pages),
        grid_spec=grid_spec,
        out_shape=jax.ShapeDtypeStruct((s, ROWS16, N_HEADS * MLA_V), F32),
        compiler_params=_cparams(("arbitrary",)), name="mla_sample",
    )(pt_flat, ql16, qp16, ckv_new, kpe_new, wuv_flat, cache_ckv, cache_kpe)


def _rope_tables(pos):
    half = MLA_ROPE // 2
    inv = jnp.power(ROPE_THETA, -jnp.arange(half, dtype=F32) / half)
    ang = pos.astype(F32)[:, None] * inv
    cos, sin = jnp.cos(ang), jnp.sin(ang)
    cos32 = jnp.concatenate([cos, cos], axis=1)
    sin32 = jnp.concatenate([-sin, sin], axis=1)
    reps = LANES // MLA_ROPE
    return jnp.tile(cos32, (1, reps)), jnp.tile(sin32, (1, reps))


def _block_diag(blocks):
    n, r, c = blocks.shape
    eye = jnp.eye(n, dtype=blocks.dtype)
    return (eye[:, None, :, None] * blocks[:, :, None, :]).reshape(n * r, n * c)


def _layer_weights(l, norm1_g, w_in, b_gate, w_branch, w_out, q_norm_g, kv_norm_g, w_uq, w_uk, w_uv,
                   conv_w, conv_b, w_rg_a, b_rg_a, w_rg_x, b_rg_x, lru_lambda, norm2_g, w_mlp1, w_mlp2):
    w = w_in[l]
    cq_pad = 256 - MLA_Q_RANK
    w_proj = jnp.concatenate([
        w[:, 0:1024],
        jnp.pad(w[:, 1024:1216], ((0, 0), (0, cq_pad))),
        w[:, 1216:1344],
        jnp.tile(w[:, 1344:1376], (1, LANES // MLA_ROPE)),
        w[:, 1376:1888]], axis=1).astype(BF16)
    w_moba = w[:, 512:896]
    w_proj_lo = (w_moba - w_moba.astype(BF16).astype(F32)).astype(BF16)
    uq = jnp.pad(w_uq[l], ((0, cq_pad), (0, 0), (0, 0)))
    uk = jnp.transpose(w_uk[l], (1, 2, 0))
    return {
        "g1": norm1_g[l][None, :], "w_proj": w_proj, "w_proj_lo": w_proj_lo,
        "qg": jnp.pad(q_norm_g[l], (0, cq_pad))[None, :], "kvg": kv_norm_g[l][None, :],
        "wuq_nope": uq[:, :, :MLA_NOPE].reshape(256, N_HEADS * MLA_NOPE).astype(BF16),
        "wuq_pe": uq[:, :, MLA_NOPE:].reshape(256, N_HEADS * MLA_ROPE).astype(BF16),
        "wuk_bd": _block_diag(uk).astype(BF16),
        "wuv_heads": jnp.transpose(w_uv[l], (1, 0, 2)).astype(BF16),
        "wuv_flat": w_uv[l].reshape(MLA_KV_RANK, N_HEADS * MLA_V).astype(BF16),
        "conv_w": conv_w[l], "conv_b": conv_b[l][None, :],
        "w_rg_a": _block_diag(w_rg_a[l]).astype(BF16), "b_rg_a": b_rg_a[l][None, :],
        "w_rg_x": _block_diag(w_rg_x[l]).astype(BF16), "b_rg_x": b_rg_x[l][None, :],
        "lam": lru_lambda[l][None, :],
        "w_gl": w[:, 1888:].astype(BF16), "b_gate": b_gate[l].reshape(1, N_BRANCH * D_MODEL),
        "w_branch": w_branch[l].astype(BF16), "w_out": w_out[l].astype(BF16),
        "g2": norm2_g[l][None, :], "w_mlp1": w_mlp1[l].astype(BF16), "w_mlp2": w_mlp2[l].astype(BF16),
    }


def _decode_consts():
    j = jnp.arange(LANES)
    return {"suffix": (j[:, None] > j[None, :]).astype(BF16)}


def _keys_on_lanes(cache):
    d, n = cache.shape[:2]
    moved = jnp.moveaxis(cache, 2, -1)
    return moved.reshape(d, n, -1, PAGE)


def _rows16(x):
    return jnp.pad(x, ((0, 0), (0, ROWS16 - x.shape[1]), (0, 0)))


def _pick_head_lanes(o16):
    parts = [o16[:, j, (j // 2) * HEAD_DIM:(j // 2 + 1) * HEAD_DIM] for j in range(N_HEADS)]
    return jnp.concatenate(parts, axis=1)


def kernel(x_prompt, x_sample, cache_sb_k, cache_sb_v, cache_moba_k, cache_moba_v, cache_mla_ckv, cache_mla_kpe,
           state_lru_h, state_lru_conv, page_table, norm1_g, w_in, b_gate, w_branch, w_out, q_norm_g, kv_norm_g,
           w_uq, w_uk, w_uv, conv_w, conv_b, w_rg_a, b_rg_a, w_rg_x, b_rg_x, lru_lambda, norm2_g, w_mlp1, w_mlp2,
           final_norm_g):
    batch, seq, _ = x_prompt.shape
    n_dec = x_sample.shape[0]
    depth = w_in.shape[0]
    n_pages = page_table.shape[1]
    past = n_pages * PAGE
    assert x_sample.shape[1] == 1 and seq % 512 == 0 and past % MB_BLOCK == 0

    tm_p = 512
    tm_s = n_dec
    nb = seq // MB_BLOCK
    nb_pad = max(LANES, nb)

    tab_p = _rope_tables(jnp.arange(seq, dtype=jnp.int32))
    tab_s = _rope_tables(jnp.full((n_dec,), past, jnp.int32))
    consts = _decode_consts()
    pt_flat = page_table.reshape(-1)
    gf = final_norm_g[None, :]

    c_sb_k, c_sb_v = _keys_on_lanes(cache_sb_k), _keys_on_lanes(cache_sb_v)
    c_mb_k, c_mb_v = _keys_on_lanes(cache_moba_k), _keys_on_lanes(cache_moba_v)
    c_kpe = _keys_on_lanes(cache_mla_kpe)

    xp = x_prompt.reshape(batch * seq, D_MODEL)
    xs = x_sample.reshape(n_dec, D_MODEL)
    p_states = [[] for _ in range(8)]
    s_states = [[] for _ in range(8)]

    for l in range(depth):
        lw = _layer_weights(l, norm1_g, w_in, b_gate, w_branch, w_out, q_norm_g, kv_norm_g, w_uq, w_uk, w_uv,
                            conv_w, conv_b, w_rg_a, b_rg_a, w_rg_x, b_rg_x, lru_lambda, norm2_g, w_mlp1, w_mlp2)
        final = l == depth - 1

        (qa, qb, qbl, ka, va, kb, vb, kab, vab, kbb, vbb, qc, _, kc, ckv, kpe, xrg, kmean) = _proj(
            xp, tab_p, lw, tm=tm_p, with_kmean=True)
        kmean = jnp.pad(kmean.reshape(batch, nb, LANES), ((0, 0), (0, nb_pad - nb), (0, 0)))
        o_a = _sb_prompt(qa, kab, vab, batch=batch, seq=seq, tq=256, tk=128)
        o_b = _moba_prompt(qb, qbl, kbb, vbb, kmean, batch=batch, seq=seq, tq=256)
        o_c = _mla_prompt(qc, kc, lw["wuv_heads"], batch=batch, seq=seq, tq=128, tk=256)
        o_d, h_last = _lru_prompt(xrg, lw, batch=batch, seq=seq, tt=512)
        xp = _mlp(_merge(xp, (o_a, o_b, o_c, o_d), lw, tm=tm_p), lw, gf, tm=tm_p, final=final)
        new_p = (ka.reshape(batch, seq, N_KV, HEAD_DIM), va.reshape(batch, seq, N_KV, HEAD_DIM),
                 kb.reshape(batch, seq, N_KV, HEAD_DIM), vb.reshape(batch, seq, N_KV, HEAD_DIM),
                 ckv.reshape(batch, seq, MLA_KV_RANK), kpe.reshape(batch, seq, MLA_ROPE),
                 h_last.reshape(batch, LRU_WIDTH),
                 xrg.reshape(batch, seq, 512)[:, seq - (CONV_W - 1):, :LRU_WIDTH])
        for lst, a in zip(p_states, new_p):
            lst.append(a)

        (qa, qb, qbl, ka, va, kb, vb, _, _, _, _, qc, qpe, _, ckv, kpe, xrg) = _proj(
            xs, tab_s, lw, tm=tm_s, with_kmean=False)
        qa16 = _rows16(qa.reshape(n_dec, N_HEADS, LANES))
        qb16 = _rows16(qb.reshape(n_dec, N_HEADS, LANES))
        qbl16 = _rows16(qbl.reshape(n_dec, N_HEADS, LANES))
        ql16 = _rows16(qc.reshape(n_dec, N_HEADS, 256)[:, :, :LANES])
        qp16 = _rows16(qpe.reshape(n_dec, N_HEADS, MLA_ROPE))
        o_a = _pick_head_lanes(_sb_sample(qa16, c_sb_k, c_sb_v, pt_flat, consts, layer=l, n_pages=n_pages,
                                          cp=SB_CHUNK_PAGES))
        o_b = _pick_head_lanes(_moba_sample(qb16, qbl16, kb[:, None, :], vb[:, None, :], c_mb_k, c_mb_v, pt_flat,
                                            layer=l, n_pages=n_pages))
        o_c16 = _mla_sample(ql16, qp16, ckv[:, None, :], kpe[:, None, :], lw["wuv_flat"], cache_mla_ckv,
                            c_kpe, pt_flat, layer=l, n_pages=n_pages)
        o_c = jnp.concatenate([o_c16[:, h, h * MLA_V:(h + 1) * MLA_V] for h in range(N_HEADS)], axis=1)
        buf0 = state_lru_conv[l]
        o_d, h_new = _lru_sample(xrg, jnp.transpose(buf0, (1, 0, 2)), state_lru_h[l], lw)
        outs = (o_a.astype(BF16), o_b.astype(BF16), o_c.astype(BF16), o_d)
        xs = _mlp(_merge(xs, outs, lw, tm=tm_s), lw, gf, tm=tm_s, final=final)
        new_s = (ka.reshape(n_dec, 1, N_KV, HEAD_DIM), va.reshape(n_dec, 1, N_KV, HEAD_DIM),
                 kb.reshape(n_dec, 1, N_KV, HEAD_DIM), vb.reshape(n_dec, 1, N_KV, HEAD_DIM),
                 ckv.reshape(n_dec, 1, MLA_KV_RANK), kpe.reshape(n_dec, 1, MLA_ROPE), h_new,
                 jnp.concatenate([buf0[:, 1:], xrg[:, None, :LRU_WIDTH]], axis=1))
        for lst, a in zip(s_states, new_s):
            lst.append(a)

    y_prompt = xp.reshape(batch, seq, D_MODEL)
    y_sample = xs.reshape(n_dec, 1, D_MODEL)
    return (y_prompt, y_sample, *[jnp.stack(s) for s in p_states], *[jnp.stack(s) for s in s_states])
```

```python
import functools

import jax
import jax.numpy as jnp
from jax import lax
from jax.experimental import pallas as pl
from jax.experimental.pallas import tpu as pltpu

F32 = jnp.float32
BF16 = jnp.bfloat16

D_MODEL = 1024
HEAD_DIM = 64
N_HEADS = 4
N_KV = 2
MB_BLOCK = 256
MB_TOPK = 3
MLA_Q_RANK = 192
MLA_KV_RANK = 128
MLA_NOPE = 64
MLA_ROPE = 32
MLA_V = 64
ROPE_THETA = 10000.0
LRU_WIDTH = 256
LRU_BLOCKS = 4
CONV_W = 4
LRU_C = 8.0
N_BRANCH = 4
BRANCH_W = 256
D_FF = 4 * D_MODEL
EPS = 1e-6
PAGE = 128

LANES = 128
ROWS16 = 16
NEG_BIG = -1e30
SB_DEAD = -104.0
SB_CHUNK_PAGES = 4
MLA_TQ, MLA_TK = 128, 256
VMEM_LIMIT = 56 * 1024 * 1024

NT = (((1,), (1,)), ((), ()))


def _dot(a, b):
    return jnp.dot(a, b, preferred_element_type=F32)


def _dot_nt(a, b):
    return lax.dot_general(a, b, NT, preferred_element_type=F32)


def _split2(x):
    hi = x.astype(BF16)
    lo = (x - hi.astype(F32)).astype(BF16)
    return hi, lo


def _split3(x):
    hi = x.astype(BF16)
    r = x - hi.astype(F32)
    mid = r.astype(BF16)
    lo = (r - mid.astype(F32)).astype(BF16)
    return hi, mid, lo


def _rms(x, g, n):
    ms = jnp.sum(x * x, axis=-1, keepdims=True) * (1.0 / n)
    return x * lax.rsqrt(ms + EPS) * g


def _softplus_neg_abs(z):
    return jnp.log(1.0 + jnp.exp(-jnp.abs(z)))


def _gelu(x):
    return 0.5 * x * (1.0 + jnp.tanh(0.7978845608028654 * (x + 0.044715 * x * x * x)))


def _sigmoid(x):
    return 1.0 / (1.0 + jnp.exp(-x))


def _cparams(sem):
    return pltpu.CompilerParams(dimension_semantics=sem, vmem_limit_bytes=VMEM_LIMIT)


def _const_spec(shape):
    nd = len(shape)
    return pl.BlockSpec(shape, lambda *_: (0,) * nd)


def _proj_body(x_ref, g1_ref, w_ref, wlo_ref, qg_ref, kvg_ref, wuqn_ref, wuqp_ref, wuk_ref, cos_ref, sin_ref,
               qa_ref, qb_ref, qbl_ref, ka_ref, va_ref, kb_ref, vb_ref, kab_ref, vab_ref, kbb_ref, vbb_ref,
               qc_ref, qpe_ref, kc_ref, ckv_ref, kpe_ref, xrg_ref, *prompt_refs, tm, prompt):
    x = x_ref[...]
    u32 = _rms(x, g1_ref[...], D_MODEL)
    u, u_lo = _split2(u32)
    acc = _dot(u, w_ref[...])
    fine = _dot(u_lo, w_ref[:, 512:896]) + _dot(u, wlo_ref[...])
    lane = lax.broadcasted_iota(jnp.int32, (tm, LANES), 1)
    lo = lane < HEAD_DIM

    def pad_heads(q):
        v0, v1 = q[:, :LANES], q[:, LANES:]
        r0, r1 = pltpu.roll(v0, HEAD_DIM, 1), pltpu.roll(v1, HEAD_DIM, 1)
        z = jnp.zeros_like(v0)
        heads = [jnp.where(lo, v0, z), jnp.where(lo, r0, z), jnp.where(lo, z, r1), jnp.where(lo, z, v1)]
        return jnp.concatenate(heads, axis=1) * (HEAD_DIM ** -0.5)

    qa_ref[...] = pad_heads(acc[:, 0:256]).astype(BF16)
    ka, va = acc[:, 256:384], acc[:, 384:512]
    qb_pad = pad_heads(acc[:, 512:768] + fine[:, :256])
    qb_hi, qb_lo = _split2(qb_pad)
    qb_ref[...] = qb_hi
    qbl_ref[...] = qb_lo
    kb, vb = acc[:, 768:896] + fine[:, 256:], acc[:, 896:1024]
    ka_ref[...] = ka
    va_ref[...] = va
    kb_ref[...] = kb
    vb_ref[...] = vb
    kab_ref[...] = ka.astype(BF16)
    vab_ref[...] = va.astype(BF16)
    kbb_ref[...] = kb.astype(BF16)
    vbb_ref[...] = vb.astype(BF16)
    if prompt:
        kmean_ref, qct_ref, vt_ref, qbt_ref, qbtl_ref, vbt_ref = prompt_refs
        kmean_ref[...] = jnp.sum(kb.reshape(tm // MB_BLOCK, MB_BLOCK, LANES), axis=1) * (1.0 / MB_BLOCK)
        qbt_hi, qbt_lo = _split2(jnp.transpose(qb_pad))
        qbt_ref[...] = qbt_hi
        qbtl_ref[...] = qbt_lo
        vb_t = jnp.transpose(vb).astype(BF16)
        for j in range(tm // MB_BLOCK):
            vbt_ref[j] = vb_t[:, j * MB_BLOCK:(j + 1) * MB_BLOCK]

    cqn = _rms(acc[:, 1024:1280], qg_ref[...], MLA_Q_RANK).astype(BF16)
    q_nope = _dot(cqn, wuqn_ref[...]).astype(BF16)
    q_pe = _dot(cqn, wuqp_ref[...])
    q_lat = _dot(q_nope, wuk_ref[...])
    cos, sin = cos_ref[...], sin_ref[...]
    first_half = (lane % MLA_ROPE) < (MLA_ROPE // 2)

    def rope(v):
        swapped = jnp.where(first_half, pltpu.roll(v, LANES - MLA_ROPE // 2, 1), pltpu.roll(v, MLA_ROPE // 2, 1))
        return v * cos + swapped * sin

    scale = (MLA_NOPE + MLA_ROPE) ** -0.5
    q_pe = rope(q_pe) * scale
    qpe_ref[...] = q_pe.astype(BF16)
    pieces = []
    for h in range(N_HEADS):
        pieces.append(q_lat[:, h * LANES:(h + 1) * LANES] * scale)
        pieces.append(jnp.where(lane // MLA_ROPE == h, q_pe, 0.0))
    qc = jnp.concatenate(pieces, axis=1)
    qc_ref[...] = qc.astype(BF16)

    ckvn = _rms(acc[:, 1280:1408], kvg_ref[...], MLA_KV_RANK)
    kpe4 = rope(acc[:, 1408:1536])
    if prompt:
        qct_ref[...] = jnp.transpose(qc).astype(BF16)
        ckv_t = jnp.transpose(ckvn).astype(BF16)
        for j in range(tm // MLA_TK):
            vt_ref[j] = ckv_t[:, j * MLA_TK:(j + 1) * MLA_TK]
    ckv_ref[...] = ckvn
    kpe_ref[...] = kpe4[:, :MLA_ROPE]
    kc_ref[...] = jnp.concatenate([ckvn, kpe4], axis=1).astype(BF16)
    xrg_ref[...] = acc[:, 1536:2048]


def _proj(x, tables, lw, *, tm, prompt):
    n = x.shape[0]
    nt = n // tm
    cos_t, sin_t = tables
    npos = cos_t.shape[0] // tm
    row = lambda w: pl.BlockSpec((tm, w), lambda i: (i, 0))
    tab = pl.BlockSpec((tm, LANES), lambda i: (i % npos, 0))
    out_shape = [
        jax.ShapeDtypeStruct((n, 512), BF16), jax.ShapeDtypeStruct((n, 512), BF16),
        jax.ShapeDtypeStruct((n, 512), BF16),
        jax.ShapeDtypeStruct((n, LANES), F32), jax.ShapeDtypeStruct((n, LANES), F32),
        jax.ShapeDtypeStruct((n, LANES), F32), jax.ShapeDtypeStruct((n, LANES), F32),
        jax.ShapeDtypeStruct((n, LANES), BF16), jax.ShapeDtypeStruct((n, LANES), BF16),
        jax.ShapeDtypeStruct((n, LANES), BF16), jax.ShapeDtypeStruct((n, LANES), BF16),
        jax.ShapeDtypeStruct((n, 1024), BF16), jax.ShapeDtypeStruct((n, LANES), BF16),
        jax.ShapeDtypeStruct((n, 256), BF16), jax.ShapeDtypeStruct((n, LANES), F32),
        jax.ShapeDtypeStruct((n, MLA_ROPE), F32), jax.ShapeDtypeStruct((n, 512), F32),
    ]
    out_specs = [row(512), row(512), row(512), row(LANES), row(LANES), row(LANES), row(LANES),
                 row(LANES), row(LANES), row(LANES), row(LANES),
                 row(1024), row(LANES), row(256), row(LANES), row(MLA_ROPE), row(512)]
    if prompt:
        nb = tm // MB_BLOCK
        out_shape += [jax.ShapeDtypeStruct((nt, nb, LANES), F32),
                      jax.ShapeDtypeStruct((N_HEADS * 256, n), BF16),
                      jax.ShapeDtypeStruct((n // MLA_TK, MLA_KV_RANK, MLA_TK), BF16),
                      jax.ShapeDtypeStruct((N_HEADS * LANES, n), BF16),
                      jax.ShapeDtypeStruct((N_HEADS * LANES, n), BF16),
                      jax.ShapeDtypeStruct((n // MB_BLOCK, LANES, MB_BLOCK), BF16)]
        out_specs += [pl.BlockSpec((None, nb, LANES), lambda i: (i, 0, 0)),
                      pl.BlockSpec((N_HEADS * 256, tm), lambda i: (0, i)),
                      pl.BlockSpec((tm // MLA_TK, MLA_KV_RANK, MLA_TK), lambda i: (i, 0, 0)),
                      pl.BlockSpec((N_HEADS * LANES, tm), lambda i: (0, i)),
                      pl.BlockSpec((N_HEADS * LANES, tm), lambda i: (0, i)),
                      pl.BlockSpec((nb, LANES, MB_BLOCK), lambda i: (i, 0, 0))]
    return pl.pallas_call(
        functools.partial(_proj_body, tm=tm, prompt=prompt),
        grid=(nt,),
        in_specs=[row(D_MODEL), _const_spec((1, D_MODEL)), _const_spec((D_MODEL, 2048)),
                  _const_spec((D_MODEL, 384)),
                  _const_spec((1, 256)), _const_spec((1, LANES)), _const_spec((256, 256)),
                  _const_spec((256, LANES)), _const_spec((256, 512)), tab, tab],
        out_specs=out_specs, out_shape=out_shape,
        compiler_params=_cparams(("parallel",)), name="proj",
    )(x, lw["g1"], lw["w_proj"], lw["w_proj_lo"], lw["qg"], lw["kvg"], lw["wuq_nope"], lw["wuq_pe"], lw["wuk_bd"],
      cos_t, sin_t)


def _head_lanes_out(acc, h, tq):
    a0, a1 = acc[:tq], acc[tq:]
    r0, r1 = pltpu.roll(a0, HEAD_DIM, 1), pltpu.roll(a1, HEAD_DIM, 1)
    lo = lax.broadcasted_iota(jnp.int32, (tq, LANES), 1) < HEAD_DIM
    first = jnp.where(h == 0, a0, r0)
    second = jnp.where(h == 0, r1, a1)
    return jnp.where(lo, first, second)


def _sb_prompt_body(q_ref, k_ref, v_ref, o_ref, *, tq, tk):
    h = pl.program_id(1)
    qi = pl.program_id(2)
    qblk = q_ref[...]
    q = jnp.concatenate([qblk[:, :LANES], qblk[:, LANES:]], axis=0)
    rows = 2 * tq
    t_pos = qi * tq + lax.broadcasted_iota(jnp.int32, (rows, tk), 0) % tq
    s_loc = lax.broadcasted_iota(jnp.int32, (rows, tk), 1)
    suffix = (lax.broadcasted_iota(jnp.int32, (tk, tk), 0) > lax.broadcasted_iota(jnp.int32, (tk, tk), 1)
              ).astype(BF16)

    def block(kb, carry, acc, masked):
        start = pl.multiple_of(kb * tk, tk)
        z = _dot_nt(q, k_ref[pl.ds(start, tk), :])
        sp = _softplus_neg_abs(z)
        log_keep = jnp.minimum(-z, 0.0) - sp
        log_beta = jnp.minimum(z, 0.0) - sp
        if masked:
            causal = (start + s_loc) < t_pos
            log_keep = jnp.where(causal, log_keep, 0.0)
        hi, lo = _split2(log_keep)
        later = _dot(hi, suffix) + _dot(lo, suffix)
        w = jnp.exp(log_beta + later + carry)
        if masked:
            w = jnp.where(causal, w, 0.0)
        acc = acc + _dot(w.astype(BF16), v_ref[pl.ds(start, tk), :])
        carry = carry + later[:, :1] + log_keep[:, :1]
        return carry, acc

    carry = jnp.zeros((rows, 1), F32)
    acc = jnp.zeros((rows, LANES), F32)
    n_diag = tq // tk
    for d in range(n_diag):
        carry, acc = block(qi * n_diag + (n_diag - 1 - d), carry, acc, True)

    def cond(c):
        kb, carry, _ = c
        return jnp.logical_and(kb >= 0, jnp.max(carry) > SB_DEAD)

    def step(c):
        kb, carry, acc = c
        carry, acc = block(kb, carry, acc, False)
        return kb - 1, carry, acc

    _, _, acc = lax.while_loop(cond, step, (qi * n_diag - 1, carry, acc))
    o_ref[...] = _head_lanes_out(acc, h, tq).astype(o_ref.dtype)


def _sb_prompt(qa_pad, kab, vab, *, batch, seq, tq, tk):
    nq = seq // tq
    return pl.pallas_call(
        functools.partial(_sb_prompt_body, tq=tq, tk=tk),
        grid=(batch, N_KV, nq),
        in_specs=[pl.BlockSpec((tq, 256), lambda b, h, i: (b * nq + i, h)),
                  pl.BlockSpec((seq, LANES), lambda b, h, i: (b, 0)),
                  pl.BlockSpec((seq, LANES), lambda b, h, i: (b, 0))],
        out_specs=pl.BlockSpec((tq, LANES), lambda b, h, i: (b * nq + i, h)),
        out_shape=jax.ShapeDtypeStruct((batch * seq, 256), BF16),
        compiler_params=_cparams(("parallel", "parallel", "arbitrary")), name="sb_prompt",
    )(qa_pad, kab, vab)


def _top3(gate, valid, axis=1):
    nb = gate.shape[axis]
    idx = lax.broadcasted_iota(jnp.int32, gate.shape, axis).astype(F32)
    g = jnp.where(valid, gate, -jnp.inf)
    sel = jnp.zeros(gate.shape, jnp.bool_)
    picks, oks = [], []
    for _ in range(MB_TOPK):
        m = jnp.max(g, axis=axis, keepdims=True)
        first = jnp.min(jnp.where(g == m, idx, float(nb)), axis=axis, keepdims=True)
        pick = idx == first
        sel = jnp.logical_or(sel, pick)
        g = jnp.where(pick, -jnp.inf, g)
        picks.append(first)
        oks.append(m > -jnp.inf)
    return jnp.logical_and(sel, valid), picks, oks


def _top3_mask(gate, valid):
    return _top3(gate, valid)[0]


def _alibi_slope(head):
    return jnp.exp2(-2.0 * (head + 1).astype(F32))


def _moba_prompt_body(qt_ref, qtl_ref, k_ref, vt_ref, km_ref, o_ref, sel_ref, *, tq, nb_pad):
    h = pl.program_id(1)
    qi = pl.program_id(2)
    tk = MB_BLOCK
    groups = N_HEADS // N_KV
    cols = groups * tq
    own = (qi * tq) // tk
    q_t = jnp.concatenate([qt_ref[g * LANES:(g + 1) * LANES, :] for g in range(groups)], axis=1)
    q_t_lo = jnp.concatenate([qtl_ref[g * LANES:(g + 1) * LANES, :] for g in range(groups)], axis=1)
    col = lax.broadcasted_iota(jnp.int32, (1, cols), 1)
    t_off = qi * tq - own * tk + col % tq
    slope = _alibi_slope(2 * h + col // tq)
    s_loc = lax.broadcasted_iota(jnp.int32, (tk, cols), 0)
    key_off = lax.broadcasted_iota(jnp.int32, (tk, 1), 0).astype(F32)
    blk = lax.broadcasted_iota(jnp.int32, (nb_pad, cols), 0)

    km_hi, km_lo = _split2(km_ref[...])
    gate = (_dot(km_hi, q_t) + _dot(km_hi, q_t_lo) + _dot(km_lo, q_t)) * (HEAD_DIM ** 0.5)
    sel_ref[...] = _top3(gate, blk < own, axis=0)[0].astype(F32)
    row_bias = slope * t_off.astype(F32)

    def scores(n):
        start = pl.multiple_of(n * tk, tk)
        return _dot(k_ref[pl.ds(start, tk), :], q_t)

    def process(s, n, c, own_block):
        m, l, acc = c
        s = s + key_off * slope
        if own_block:
            s = jnp.where(s_loc <= t_off, s, NEG_BIG)
            shift = row_bias
        else:
            chosen = sel_ref[pl.ds(n, 1), :] > 0.5
            gap = jnp.full((1, 1), (own - n) * tk, jnp.int32).astype(F32)
            shift = row_bias + slope * gap + jnp.where(chosen, 0.0, -NEG_BIG)
        m_new = jnp.maximum(m, jnp.max(s, axis=0, keepdims=True) - shift)
        alpha = jnp.exp(m - m_new)
        p = jnp.exp(s - (shift + m_new))
        l = alpha * l + jnp.sum(p, axis=0, keepdims=True)
        acc = alpha * acc + _dot(vt_ref[n], p.astype(BF16))
        return m_new, l, acc

    init = (jnp.full((1, cols), NEG_BIG, F32), jnp.zeros((1, cols), F32), jnp.zeros((LANES, cols), F32))
    state = process(scores(own), own, init, True)

    def body(n, c):
        s_cur, state = c
        s_next = scores(jnp.minimum(n + 1, own))
        return s_next, process(s_cur, n, state, False)

    _, (m, l, acc) = lax.fori_loop(0, own, body, (scores(0), state))
    o_t = acc / l
    mine = jnp.where(h == 0, o_t[:HEAD_DIM], o_t[HEAD_DIM:])
    o_gd = jnp.concatenate([mine[:, g * tq:(g + 1) * tq] for g in range(groups)], axis=0)
    o_ref[...] = jnp.transpose(o_gd).astype(o_ref.dtype)


def _moba_prompt(qb_t, qb_t_lo, kbb, vb_t, kmean, *, batch, seq, tq):
    nq = seq // tq
    nkb = seq // MB_BLOCK
    nb_pad = kmean.shape[1]
    groups = N_HEADS // N_KV
    q_spec = pl.BlockSpec((groups * LANES, tq), lambda b, h, i: (h, b * nq + i))
    return pl.pallas_call(
        functools.partial(_moba_prompt_body, tq=tq, nb_pad=nb_pad),
        grid=(batch, N_KV, nq),
        in_specs=[q_spec, q_spec,
                  pl.BlockSpec((seq, LANES), lambda b, h, i: (b, 0)),
                  pl.BlockSpec((nkb, LANES, MB_BLOCK), lambda b, h, i: (b, 0, 0)),
                  pl.BlockSpec((None, nb_pad, LANES), lambda b, h, i: (b, 0, 0))],
        out_specs=pl.BlockSpec((tq, LANES), lambda b, h, i: (b * nq + i, h)),
        out_shape=jax.ShapeDtypeStruct((batch * seq, 256), BF16),
        scratch_shapes=[pltpu.VMEM((nb_pad, groups * tq), F32)],
        compiler_params=_cparams(("parallel", "parallel", "arbitrary")), name="moba_prompt",
    )(qb_t, qb_t_lo, kbb, vb_t, kmean)


def _mla_prompt_body(qt_ref, kc_ref, vt_ref, wuv_ref, o_ref, *, tq, tk):
    qi = pl.program_id(1)
    cols = N_HEADS * tq
    q_t = jnp.concatenate([qt_ref[h * 256:(h + 1) * 256, :] for h in range(N_HEADS)], axis=1)
    s_loc = lax.broadcasted_iota(jnp.int32, (tk, cols), 0)
    t_loc = lax.broadcasted_iota(jnp.int32, (tk, cols), 1) % tq
    diag = (qi * tq) // tk

    def scores(kb):
        start = pl.multiple_of(kb * tk, tk)
        return _dot(kc_ref[pl.ds(start, tk), :], q_t)

    def process(s, kb, c, masked):
        m, l, acc = c
        if masked:
            s = jnp.where(kb * tk + s_loc <= qi * tq + t_loc, s, NEG_BIG)
        m_new = jnp.maximum(m, jnp.max(s, axis=0, keepdims=True))
        alpha = jnp.exp(m - m_new)
        p = jnp.exp(s - m_new)
        l = alpha * l + jnp.sum(p, axis=0, keepdims=True)
        acc = alpha * acc + _dot(vt_ref[kb], p.astype(BF16))
        return m_new, l, acc

    def body(i, c):
        s_cur, state = c
        s_next = scores(i + 1)
        return s_next, process(s_cur, i, state, False)

    init = (jnp.full((1, cols), NEG_BIG, F32), jnp.zeros((1, cols), F32), jnp.zeros((MLA_KV_RANK, cols), F32))
    s_cur, state = lax.fori_loop(0, diag, body, (scores(0), init))
    m, l, acc = process(s_cur, diag, state, True)
    o_lat_t = (acc / l).astype(BF16)
    contract_rows = (((0,), (0,)), ((), ()))
    outs = [lax.dot_general(o_lat_t[:, h * tq:(h + 1) * tq], wuv_ref[h], contract_rows, preferred_element_type=F32)
            for h in range(N_HEADS)]
    o_ref[...] = jnp.concatenate(outs, axis=1).astype(o_ref.dtype)


def _mla_prompt(qc_t, kc, v_t, wuv, *, batch, seq, tq, tk):
    nq = seq // tq
    nkb = seq // tk
    return pl.pallas_call(
        functools.partial(_mla_prompt_body, tq=tq, tk=tk),
        grid=(batch, nq),
        in_specs=[pl.BlockSpec((N_HEADS * 256, tq), lambda b, i: (0, b * nq + i)),
                  pl.BlockSpec((seq, 256), lambda b, i: (b, 0)),
                  pl.BlockSpec((nkb, MLA_KV_RANK, tk), lambda b, i: (b, 0, 0)),
                  _const_spec((N_HEADS, MLA_KV_RANK, MLA_V))],
        out_specs=pl.BlockSpec((tq, 256), lambda b, i: (b * nq + i, 0)),
        out_shape=jax.ShapeDtypeStruct((batch * seq, 256), BF16),
        compiler_params=_cparams(("parallel", "arbitrary")), name="mla_prompt",
    )(qc_t, kc, v_t, wuv)


def _lru_gates(xc, wa_ref, ba_ref, wx_ref, bx_ref, lam_ref):
    xb = xc.astype(BF16)
    r = _sigmoid(_dot(xb, wa_ref[...]) + ba_ref[...])
    i = _sigmoid(_dot(xb, wx_ref[...]) + bx_ref[...])
    lam = lam_ref[...]
    softplus_neg_lam = jnp.maximum(-lam, 0.0) + _softplus_neg_abs(lam)
    log_a = -LRU_C * r * softplus_neg_lam
    a = jnp.exp(log_a)
    th = jnp.tanh(log_a)
    u = jnp.sqrt(-2.0 * th / (1.0 - th)) * i * xc
    return a, u


def _lru_prompt_body(xrg_ref, cw_ref, cb_ref, wa_ref, ba_ref, wx_ref, bx_ref, lam_ref,
                     od_ref, hlast_ref, xpad, hprev, *, tt):
    t = pl.program_id(1)

    @pl.when(t == 0)
    def _():
        xpad[0:8, :] = jnp.zeros((8, LRU_WIDTH), F32)
        hprev[...] = jnp.zeros((1, LRU_WIDTH), F32)

    x = xrg_ref[:, :LRU_WIDTH]
    xg = xrg_ref[:, LRU_WIDTH:]
    xpad[8:8 + tt, :] = x
    xc = cb_ref[...] + cw_ref[CONV_W - 1:CONV_W, :] * x
    for i in range(CONV_W - 1):
        back = CONV_W - 1 - i
        xc = xc + cw_ref[i:i + 1, :] * xpad[8 - back:8 - back + tt, :]
    xpad[0:8, :] = x[tt - 8:, :]

    a, u = _lru_gates(xc, wa_ref, ba_ref, wx_ref, bx_ref, lam_ref)
    row = lax.broadcasted_iota(jnp.int32, (tt, LRU_WIDTH), 0)
    d = 1
    while d < tt:
        keep = row >= d
        a_sh = jnp.where(keep, pltpu.roll(a, d, 0), 1.0)
        u_sh = jnp.where(keep, pltpu.roll(u, d, 0), 0.0)
        u = u + a * u_sh
        a = a * a_sh
        d *= 2
    hs = a * hprev[...] + u
    hprev[...] = hs[tt - 1:, :]
    hlast_ref[...] = hs[tt - 1:, :]
    od_ref[...] = (hs * _gelu(xg)).astype(od_ref.dtype)


def _lru_prompt(xrg, lw, *, batch, seq, tt):
    nt = seq // tt
    vec = _const_spec((1, LRU_WIDTH))
    mat = _const_spec((LRU_WIDTH, LRU_WIDTH))
    return pl.pallas_call(
        functools.partial(_lru_prompt_body, tt=tt),
        grid=(batch, nt),
        in_specs=[pl.BlockSpec((tt, 512), lambda b, t: (b * nt + t, 0)),
                  _const_spec((CONV_W, LRU_WIDTH)), vec, mat, vec, mat, vec, vec],
        out_specs=[pl.BlockSpec((tt, LRU_WIDTH), lambda b, t: (b * nt + t, 0)),
                   pl.BlockSpec((None, 1, LRU_WIDTH), lambda b, t: (b, 0, 0))],
        out_shape=[jax.ShapeDtypeStruct((batch * seq, LRU_WIDTH), BF16),
                   jax.ShapeDtypeStruct((batch, 1, LRU_WIDTH), F32)],
        scratch_shapes=[pltpu.VMEM((tt + 8, LRU_WIDTH), F32), pltpu.VMEM((1, LRU_WIDTH), F32)],
        compiler_params=_cparams(("parallel", "arbitrary")), name="lru_prompt",
    )(xrg, lw["conv_w"], lw["conv_b"], lw["w_rg_a"], lw["b_rg_a"], lw["w_rg_x"], lw["b_rg_x"], lw["lam"])


def _lru_sample_body(xrg_ref, buf_ref, h0_ref, cw_ref, cb_ref, wa_ref, ba_ref, wx_ref, bx_ref, lam_ref,
                     od_ref, h_ref):
    x = xrg_ref[:, :LRU_WIDTH]
    xg = xrg_ref[:, LRU_WIDTH:]
    xc = cb_ref[...] + cw_ref[CONV_W - 1:CONV_W, :] * x
    for i in range(CONV_W - 1):
        xc = xc + cw_ref[i:i + 1, :] * buf_ref[i]
    a, u = _lru_gates(xc, wa_ref, ba_ref, wx_ref, bx_ref, lam_ref)
    h = a * h0_ref[...] + u
    h_ref[...] = h
    od_ref[...] = (h * _gelu(xg)).astype(od_ref.dtype)


def _lru_sample(xrg, buf_t, h0, lw):
    s = xrg.shape[0]
    vec = _const_spec((1, LRU_WIDTH))
    mat = _const_spec((LRU_WIDTH, LRU_WIDTH))
    return pl.pallas_call(
        _lru_sample_body,
        grid=(1,),
        in_specs=[_const_spec((s, 512)), _const_spec((CONV_W - 1, s, LRU_WIDTH)), _const_spec((s, LRU_WIDTH)),
                  _const_spec((CONV_W, LRU_WIDTH)), vec, mat, vec, mat, vec, vec],
        out_specs=[_const_spec((s, LRU_WIDTH)), _const_spec((s, LRU_WIDTH))],
        out_shape=[jax.ShapeDtypeStruct((s, LRU_WIDTH), BF16), jax.ShapeDtypeStruct((s, LRU_WIDTH), F32)],
        compiler_params=_cparams(("arbitrary",)), name="lru_sample",
    )(xrg, buf_t, h0, lw["conv_w"], lw["conv_b"], lw["w_rg_a"], lw["b_rg_a"], lw["w_rg_x"], lw["b_rg_x"],
      lw["lam"])


def _merge_body(x_ref, oa_ref, ob_ref, oc_ref, od_ref, g1_ref, wgl_ref, bg_ref, wbr_ref, wout_ref, y_ref):
    x = x_ref[...]
    u = _rms(x, g1_ref[...], D_MODEL).astype(BF16)
    s = jnp.zeros(x.shape, F32)
    for k, o_ref in enumerate((oa_ref, ob_ref, oc_ref, od_ref)):
        cols = slice(k * D_MODEL, (k + 1) * D_MODEL)
        gate = _sigmoid(_dot(u, wgl_ref[:, cols]) + bg_ref[:, cols])
        s = s + gate * _dot(o_ref[...], wbr_ref[k])
    y_ref[...] = x + _dot(s.astype(BF16), wout_ref[...])


def _merge(x, outs, lw, *, tm):
    n = x.shape[0]
    row = lambda w: pl.BlockSpec((tm, w), lambda i: (i, 0))
    return pl.pallas_call(
        _merge_body,
        grid=(n // tm,),
        in_specs=[row(D_MODEL), row(256), row(256), row(256), row(256), _const_spec((1, D_MODEL)),
                  _const_spec((D_MODEL, N_BRANCH * D_MODEL)), _const_spec((1, N_BRANCH * D_MODEL)),
                  _const_spec((N_BRANCH, BRANCH_W, D_MODEL)), _const_spec((D_MODEL, D_MODEL))],
        out_specs=row(D_MODEL),
        out_shape=jax.ShapeDtypeStruct((n, D_MODEL), F32),
        compiler_params=_cparams(("parallel",)), name="merge",
    )(x, *outs, lw["g1"], lw["w_gl"], lw["b_gate"], lw["w_branch"], lw["w_out"])


def _mlp_body(x_ref, g2_ref, w1_ref, w2_ref, gf_ref, y_ref, *, final, chunk):
    x = x_ref[...]
    n = _rms(x, g2_ref[...], D_MODEL).astype(BF16)
    acc = x
    for c in range(D_FF // chunk):
        cols = slice(c * chunk, (c + 1) * chunk)
        h = jnp.maximum(_dot(n, w1_ref[:, cols]), 0.0)
        acc = acc + _dot((h * h).astype(BF16), w2_ref[cols, :])
    if final:
        acc = _rms(acc, gf_ref[...], D_MODEL)
    y_ref[...] = acc


def _mlp(x, lw, gf, *, tm, final):
    n = x.shape[0]
    row = pl.BlockSpec((tm, D_MODEL), lambda i: (i, 0))
    return pl.pallas_call(
        functools.partial(_mlp_body, final=final, chunk=1024),
        grid=(n // tm,),
        in_specs=[row, _const_spec((1, D_MODEL)), _const_spec((D_MODEL, D_FF)), _const_spec((D_FF, D_MODEL)),
                  _const_spec((1, D_MODEL))],
        out_specs=row,
        out_shape=jax.ShapeDtypeStruct((n, D_MODEL), F32),
        compiler_params=_cparams(("parallel",)), name="mlp",
    )(x, lw["g2"], lw["w_mlp1"], lw["w_mlp2"], gf)


def _page_copy(cache, layer, pt_ref, s, p, n_pages, buf, slot, sem, layout):
    page = pt_ref[s * n_pages + p]
    if layout == "lanes":
        dst = buf.at[slot, :, pl.ds(pl.multiple_of(p * PAGE, PAGE), PAGE)]
    elif layout == "blocks":
        per = MB_BLOCK // PAGE
        dst = buf.at[slot, p // per, :, pl.ds(pl.multiple_of((p % per) * PAGE, PAGE), PAGE)]
    else:
        dst = buf.at[slot, pl.ds(pl.multiple_of(p * PAGE, PAGE), PAGE)]
    return pltpu.make_async_copy(cache.at[layer, page], dst, sem.at[slot])


def _paged_prefetch(caches, bufs, sems, layouts, pt_ref, layer, n_pages):
    s = pl.program_id(0)
    ns = pl.num_programs(0)
    streams = tuple(zip(caches, bufs, sems, layouts))

    def copies(sample, slot):
        return [_page_copy(cache, layer, pt_ref, sample, p, n_pages, buf, slot, sem, kl)
                for p in range(n_pages) for cache, buf, sem, kl in streams]

    @pl.when(s == 0)
    def _():
        for c in copies(0, 0):
            c.start()

    slot = s % 2
    nxt, nxt_slot = jnp.minimum(s + 1, ns - 1), (s + 1) % 2
    for c in copies(s, slot):
        c.wait()
    for c in copies(nxt, nxt_slot):
        c.start()

    def drain():
        @pl.when(s == ns - 1)
        def _():
            for c in copies(nxt, nxt_slot):
                c.wait()

    return slot, drain


def _to_blocks(x, nblk):
    return jnp.concatenate([x[:, b * LANES:(b + 1) * LANES] for b in range(nblk)], axis=0)


def _from_blocks(x, nblk):
    r = x.shape[0] // nblk
    return jnp.concatenate([x[b * r:(b + 1) * r] for b in range(nblk)], axis=1)


def _sb_sample_body(pt_ref, q_ref, suffix_ref, kc_hbm, vc_hbm, o_ref, kbuf, vbuf, kmore, vmore, sems, more_sems,
                    *, layer, n_pages, cp):
    s = pl.program_id(0)
    ns = pl.num_programs(0)
    n_chunks = n_pages // cp

    def chunk_copies(sample, j, kdst, vdst, ksem, vsem):
        copies = []
        for i in range(cp):
            page = pt_ref[sample * n_pages + n_pages - (j + 1) * cp + i]
            lanes = pl.ds(i * PAGE, PAGE)
            copies.append(pltpu.make_async_copy(kc_hbm.at[layer, page], kdst.at[:, lanes], ksem))
            copies.append(pltpu.make_async_copy(vc_hbm.at[layer, page], vdst.at[:, lanes], vsem))
        return copies

    def first_chunk(sample, slot):
        return chunk_copies(sample, 0, kbuf.at[slot], vbuf.at[slot], sems.at[0, slot], sems.at[1, slot])

    @pl.when(s == 0)
    def _():
        for c in first_chunk(0, 0):
            c.start()

    @pl.when(s + 1 < ns)
    def _():
        for c in first_chunk(s + 1, (s + 1) % 2):
            c.start()

    slot = s % 2
    for c in first_chunk(s, slot):
        c.wait()

    q = q_ref[...]
    suffix = suffix_ref[...]
    r = lax.broadcasted_iota(jnp.int32, (8 * cp, 8 * cp), 0)
    c_ = lax.broadcasted_iota(jnp.int32, (8 * cp, 8 * cp), 1)
    later_pages = jnp.logical_and(c_ % 8 == r % 8, c_ > r).astype(BF16)

    def process(k_t, v_t, carry, acc):
        z = _dot(q, k_t.astype(BF16))[:8]
        zb = _to_blocks(z, cp)
        sp = _softplus_neg_abs(zb)
        log_keep = jnp.minimum(-zb, 0.0) - sp
        log_beta = jnp.minimum(zb, 0.0) - sp
        hi, lo = _split2(log_keep)
        later = _dot(hi, suffix) + _dot(lo, suffix)
        total = jnp.broadcast_to(later[:, :1] + log_keep[:, :1], later.shape)
        t_hi, t_mid, t_lo = _split3(total)
        cross = _dot(later_pages, t_hi) + _dot(later_pages, t_mid) + _dot(later_pages, t_lo)
        carry_rows = jnp.concatenate([carry] * cp, axis=0)
        w = _from_blocks(jnp.exp(log_beta + later + cross + carry_rows), cp)
        w16 = jnp.concatenate([w, jnp.zeros_like(w)], axis=0).astype(BF16)
        acc = acc + _dot_nt(w16, v_t.astype(BF16))
        carry = carry + cross[:8, :1] + total[:8, :1]
        return carry, acc

    carry, acc = process(kbuf[slot], vbuf[slot], jnp.zeros((8, 1), F32), jnp.zeros((ROWS16, LANES), F32))

    def cond(c):
        j, carry, _ = c
        return jnp.logical_and(j < n_chunks, jnp.max(carry[:N_HEADS]) > SB_DEAD)

    def body(c):
        j, carry, acc = c
        copies = chunk_copies(s, j, kmore, vmore, more_sems.at[0], more_sems.at[1])
        for cpy in copies:
            cpy.start()
        for cpy in copies:
            cpy.wait()
        carry, acc = process(kmore[...], vmore[...], carry, acc)
        return j + 1, carry, acc

    _, _, acc = lax.while_loop(cond, body, (1, carry, acc))
    o_ref[...] = acc


def _sb_sample(q16, cache_k, cache_v, pt_flat, consts, *, layer, n_pages, cp):
    s = q16.shape[0]
    assert n_pages % cp == 0
    grid_spec = pltpu.PrefetchScalarGridSpec(
        num_scalar_prefetch=1, grid=(s,),
        in_specs=[pl.BlockSpec((None, ROWS16, LANES), lambda i, pt: (i, 0, 0)),
                  pl.BlockSpec((LANES, LANES), lambda i, pt: (0, 0)),
                  pl.BlockSpec(memory_space=pl.ANY), pl.BlockSpec(memory_space=pl.ANY)],
        out_specs=pl.BlockSpec((None, ROWS16, LANES), lambda i, pt: (i, 0, 0)),
        scratch_shapes=[pltpu.VMEM((2, LANES, cp * PAGE), F32), pltpu.VMEM((2, LANES, cp * PAGE), F32),
                        pltpu.VMEM((LANES, cp * PAGE), F32), pltpu.VMEM((LANES, cp * PAGE), F32),
                        pltpu.SemaphoreType.DMA((2, 2)), pltpu.SemaphoreType.DMA((2,))])
    return pl.pallas_call(
        functools.partial(_sb_sample_body, layer=layer, n_pages=n_pages, cp=cp),
        grid_spec=grid_spec,
        out_shape=jax.ShapeDtypeStruct((s, ROWS16, LANES), F32),
        compiler_params=_cparams(("arbitrary",)), name="sb_sample",
    )(pt_flat, q16, consts["suffix"], cache_k, cache_v)


def _moba_sample_body(pt_ref, q_ref, ql_ref, knew_ref, vnew_ref, kc_hbm, vc_hbm, o_ref, kbuf, vbuf, sems,
                      *, layer, n_pages):
    slot, drain = _paged_prefetch((kc_hbm, vc_hbm), (kbuf, vbuf), (sems.at[0], sems.at[1]), ("blocks", "blocks"),
                                  pt_ref, layer, n_pages)
    past = n_pages * PAGE
    nb = past // MB_BLOCK
    q = q_ref[...]
    kmean = jnp.sum(kbuf[slot], axis=2) * (1.0 / MB_BLOCK)
    km_hi, km_lo = _split2(kmean)
    gate = (_dot_nt(q, km_hi) + _dot_nt(ql_ref[...], km_hi) + _dot_nt(q, km_lo)) * (HEAD_DIM ** 0.5)
    _, picks, oks = _top3(gate, jnp.ones(gate.shape, jnp.bool_))

    head = lax.broadcasted_iota(jnp.int32, (ROWS16, 1), 0)
    slope = _alibi_slope(head)
    key_off = lax.broadcasted_iota(jnp.int32, (ROWS16, MB_BLOCK), 1)
    chosen = [[jnp.minimum(picks[i][j, 0].astype(jnp.int32), nb - 1) for j in range(N_HEADS)]
              for i in range(MB_TOPK)]

    tiles = []
    for i in range(MB_TOPK):
        tile = jnp.zeros((ROWS16, MB_BLOCK), F32)
        for j in range(N_HEADS):
            n = chosen[i][j]
            dist = (past - n * MB_BLOCK) - key_off
            s_ij = _dot(q, kbuf[slot, n].astype(BF16)) - slope * dist.astype(F32)
            tile = jnp.where(head == j, s_ij, tile)
        tiles.append(jnp.where(oks[i], tile, NEG_BIG))
    s_new = jnp.sum(q.astype(F32) * knew_ref[...], axis=1, keepdims=True)
    m = s_new
    for tile in tiles:
        m = jnp.maximum(m, jnp.max(tile, axis=1, keepdims=True))
    p_new = jnp.exp(s_new - m)
    l = p_new
    o = p_new * vnew_ref[...]
    for i in range(MB_TOPK):
        p = jnp.exp(tiles[i] - m)
        l = l + jnp.sum(p, axis=1, keepdims=True)
        for j in range(N_HEADS):
            p_j = jnp.where(head == j, p, 0.0).astype(BF16)
            o = o + _dot_nt(p_j, vbuf[slot, chosen[i][j]].astype(BF16))
    o_ref[...] = o / l
    drain()


def _moba_sample(q16, q16_lo, k_new, v_new, cache_k, cache_v, pt_flat, *, layer, n_pages):
    s = q16.shape[0]
    nb = n_pages * PAGE // MB_BLOCK
    assert nb >= MB_TOPK and n_pages * PAGE % MB_BLOCK == 0
    grid_spec = pltpu.PrefetchScalarGridSpec(
        num_scalar_prefetch=1, grid=(s,),
        in_specs=[pl.BlockSpec((None, ROWS16, LANES), lambda i, pt: (i, 0, 0)),
                  pl.BlockSpec((None, ROWS16, LANES), lambda i, pt: (i, 0, 0)),
                  pl.BlockSpec((None, 1, LANES), lambda i, pt: (i, 0, 0)),
                  pl.BlockSpec((None, 1, LANES), lambda i, pt: (i, 0, 0)),
                  pl.BlockSpec(memory_space=pl.ANY), pl.BlockSpec(memory_space=pl.ANY)],
        out_specs=pl.BlockSpec((None, ROWS16, LANES), lambda i, pt: (i, 0, 0)),
        scratch_shapes=[pltpu.VMEM((2, nb, LANES, MB_BLOCK), F32), pltpu.VMEM((2, nb, LANES, MB_BLOCK), F32),
                        pltpu.SemaphoreType.DMA((2, 2))])
    return pl.pallas_call(
        functools.partial(_moba_sample_body, layer=layer, n_pages=n_pages),
        grid_spec=grid_spec,
        out_shape=jax.ShapeDtypeStruct((s, ROWS16, LANES), F32),
        compiler_params=_cparams(("arbitrary",)), name="moba_sample",
    )(pt_flat, q16, q16_lo, k_new, v_new, cache_k, cache_v)


def _mla_sample_body(pt_ref, ql_ref, qp_ref, cnew_ref, pnew_ref, wuv_ref, ckv_hbm, kpe_hbm, o_ref,
                     cbuf, pbuf, sems, *, layer, n_pages):
    slot, drain = _paged_prefetch((ckv_hbm, kpe_hbm), (cbuf, pbuf), (sems.at[0], sems.at[1]), ("rows", "lanes"),
                                  pt_ref, layer, n_pages)
    ql = ql_ref[...]
    qp = qp_ref[...]
    cb = cbuf[slot].astype(BF16)
    s = _dot_nt(ql, cb) + _dot(qp, pbuf[slot].astype(BF16))
    s_new = (jnp.sum(ql.astype(F32) * cnew_ref[...], axis=1, keepdims=True)
             + jnp.sum(qp.astype(F32) * pnew_ref[...], axis=1, keepdims=True))
    m = jnp.maximum(jnp.max(s, axis=1, keepdims=True), s_new)
    p = jnp.exp(s - m)
    p_new = jnp.exp(s_new - m)
    l = jnp.sum(p, axis=1, keepdims=True) + p_new
    o_lat = (_dot(p.astype(BF16), cb) + p_new * cnew_ref[...]) / l
    o_ref[...] = _dot(o_lat.astype(BF16), wuv_ref[...])
    drain()


def _mla_sample(ql16, qp16, ckv_new, kpe_new, wuv_flat, cache_ckv, cache_kpe, pt_flat, *, layer, n_pages):
    s = ql16.shape[0]
    t = n_pages * PAGE
    grid_spec = pltpu.PrefetchScalarGridSpec(
        num_scalar_prefetch=1, grid=(s,),
        in_specs=[pl.BlockSpec((None, ROWS16, LANES), lambda i, pt: (i, 0, 0)),
                  pl.BlockSpec((None, ROWS16, MLA_ROPE), lambda i, pt: (i, 0, 0)),
                  pl.BlockSpec((None, 1, LANES), lambda i, pt: (i, 0, 0)),
                  pl.BlockSpec((None, 1, MLA_ROPE), lambda i, pt: (i, 0, 0)),
                  pl.BlockSpec((MLA_KV_RANK, N_HEADS * MLA_V), lambda i, pt: (0, 0)),
                  pl.BlockSpec(memory_space=pl.ANY), pl.BlockSpec(memory_space=pl.ANY)],
        out_specs=pl.BlockSpec((None, ROWS16, N_HEADS * MLA_V), lambda i, pt: (i, 0, 0)),
        scratch_shapes=[pltpu.VMEM((2, t, LANES), F32), pltpu.VMEM((2, MLA_ROPE, t), F32),
                        pltpu.SemaphoreType.DMA((2, 2))])
    return pl.pallas_call(
        functools.partial(_mla_sample_body, layer=layer, n_pages=n_pages),
        grid_spec=grid_spec,
        out_shape=jax.ShapeDtypeStruct((s, ROWS16, N_HEADS * MLA_V), F32),
        compiler_params=_cparams(("arbitrary",)), name="mla_sample",
    )(pt_flat, ql16, qp16, ckv_new, kpe_new, wuv_flat, cache_ckv, cache_kpe)


def _rope_tables(pos):
    half = MLA_ROPE // 2
    inv = jnp.power(ROPE_THETA, -jnp.arange(half, dtype=F32) / half)
    ang = pos.astype(F32)[:, None] * inv
    cos, sin = jnp.cos(ang), jnp.sin(ang)
    cos32 = jnp.concatenate([cos, cos], axis=1)
    sin32 = jnp.concatenate([-sin, sin], axis=1)
    reps = LANES // MLA_ROPE
    return jnp.tile(cos32, (1, reps)), jnp.tile(sin32, (1, reps))


def _block_diag(blocks):
    n, r, c = blocks.shape
    eye = jnp.eye(n, dtype=blocks.dtype)
    return (eye[:, None, :, None] * blocks[:, :, None, :]).reshape(n * r, n * c)


def _layer_weights(l, norm1_g, w_in, b_gate, w_branch, w_out, q_norm_g, kv_norm_g, w_uq, w_uk, w_uv,
                   conv_w, conv_b, w_rg_a, b_rg_a, w_rg_x, b_rg_x, lru_lambda, norm2_g, w_mlp1, w_mlp2):
    w = w_in[l]
    cq_pad = 256 - MLA_Q_RANK
    w_proj = jnp.concatenate([
        w[:, 0:1024],
        jnp.pad(w[:, 1024:1216], ((0, 0), (0, cq_pad))),
        w[:, 1216:1344],
        jnp.tile(w[:, 1344:1376], (1, LANES // MLA_ROPE)),
        w[:, 1376:1888]], axis=1).astype(BF16)
    w_moba = w[:, 512:896]
    w_proj_lo = (w_moba - w_moba.astype(BF16).astype(F32)).astype(BF16)
    uq = jnp.pad(w_uq[l], ((0, cq_pad), (0, 0), (0, 0)))
    uk = jnp.transpose(w_uk[l], (1, 2, 0))
    return {
        "g1": norm1_g[l][None, :], "w_proj": w_proj, "w_proj_lo": w_proj_lo,
        "qg": jnp.pad(q_norm_g[l], (0, cq_pad))[None, :], "kvg": kv_norm_g[l][None, :],
        "wuq_nope": uq[:, :, :MLA_NOPE].reshape(256, N_HEADS * MLA_NOPE).astype(BF16),
        "wuq_pe": uq[:, :, MLA_NOPE:].reshape(256, N_HEADS * MLA_ROPE).astype(BF16),
        "wuk_bd": _block_diag(uk).astype(BF16),
        "wuv_heads": jnp.transpose(w_uv[l], (1, 0, 2)).astype(BF16),
        "wuv_flat": w_uv[l].reshape(MLA_KV_RANK, N_HEADS * MLA_V).astype(BF16),
        "conv_w": conv_w[l], "conv_b": conv_b[l][None, :],
        "w_rg_a": _block_diag(w_rg_a[l]).astype(BF16), "b_rg_a": b_rg_a[l][None, :],
        "w_rg_x": _block_diag(w_rg_x[l]).astype(BF16), "b_rg_x": b_rg_x[l][None, :],
        "lam": lru_lambda[l][None, :],
        "w_gl": w[:, 1888:].astype(BF16), "b_gate": b_gate[l].reshape(1, N_BRANCH * D_MODEL),
        "w_branch": w_branch[l].astype(BF16), "w_out": w_out[l].astype(BF16),
        "g2": norm2_g[l][None, :], "w_mlp1": w_mlp1[l].astype(BF16), "w_mlp2": w_mlp2[l].astype(BF16),
    }


def _decode_consts():
    j = jnp.arange(LANES)
    return {"suffix": (j[:, None] > j[None, :]).astype(BF16)}


def _keys_on_lanes(cache):
    d, n = cache.shape[:2]
    moved = jnp.moveaxis(cache, 2, -1)
    return moved.reshape(d, n, -1, PAGE)


def _rows16(x):
    return jnp.pad(x, ((0, 0), (0, ROWS16 - x.shape[1]), (0, 0)))


def _pick_head_lanes(o16):
    parts = [o16[:, j, (j // 2) * HEAD_DIM:(j // 2 + 1) * HEAD_DIM] for j in range(N_HEADS)]
    return jnp.concatenate(parts, axis=1)


def kernel(x_prompt, x_sample, cache_sb_k, cache_sb_v, cache_moba_k, cache_moba_v, cache_mla_ckv, cache_mla_kpe,
           state_lru_h, state_lru_conv, page_table, norm1_g, w_in, b_gate, w_branch, w_out, q_norm_g, kv_norm_g,
           w_uq, w_uk, w_uv, conv_w, conv_b, w_rg_a, b_rg_a, w_rg_x, b_rg_x, lru_lambda, norm2_g, w_mlp1, w_mlp2,
           final_norm_g):
    batch, seq, _ = x_prompt.shape
    n_dec = x_sample.shape[0]
    depth = w_in.shape[0]
    n_pages = page_table.shape[1]
    past = n_pages * PAGE
    assert x_sample.shape[1] == 1 and seq % 512 == 0 and past % MB_BLOCK == 0

    tm_p = 512
    tm_s = n_dec
    nb = seq // MB_BLOCK
    nb_pad = max(LANES, nb)

    tab_p = _rope_tables(jnp.arange(seq, dtype=jnp.int32))
    tab_s = _rope_tables(jnp.full((n_dec,), past, jnp.int32))
    consts = _decode_consts()
    pt_flat = page_table.reshape(-1)
    gf = final_norm_g[None, :]

    c_sb_k, c_sb_v = _keys_on_lanes(cache_sb_k), _keys_on_lanes(cache_sb_v)
    c_mb_k, c_mb_v = _keys_on_lanes(cache_moba_k), _keys_on_lanes(cache_moba_v)
    c_kpe = _keys_on_lanes(cache_mla_kpe)

    xp = x_prompt.reshape(batch * seq, D_MODEL)
    xs = x_sample.reshape(n_dec, D_MODEL)
    p_states = [[] for _ in range(8)]
    s_states = [[] for _ in range(8)]

    for l in range(depth):
        lw = _layer_weights(l, norm1_g, w_in, b_gate, w_branch, w_out, q_norm_g, kv_norm_g, w_uq, w_uk, w_uv,
                            conv_w, conv_b, w_rg_a, b_rg_a, w_rg_x, b_rg_x, lru_lambda, norm2_g, w_mlp1, w_mlp2)
        final = l == depth - 1

        (qa, _, _, ka, va, kb, vb, kab, vab, kbb, _, _, _, kc, ckv, kpe, xrg,
         kmean, qc_t, ckv_t, qb_t, qb_t_lo, vb_t) = _proj(xp, tab_p, lw, tm=tm_p, prompt=True)
        kmean = jnp.pad(kmean.reshape(batch, nb, LANES), ((0, 0), (0, nb_pad - nb), (0, 0)))
        o_a = _sb_prompt(qa, kab, vab, batch=batch, seq=seq, tq=256, tk=128)
        o_b = _moba_prompt(qb_t, qb_t_lo, kbb, vb_t, kmean, batch=batch, seq=seq, tq=256)
        o_c = _mla_prompt(qc_t, kc, ckv_t, lw["wuv_heads"], batch=batch, seq=seq, tq=MLA_TQ, tk=MLA_TK)
        o_d, h_last = _lru_prompt(xrg, lw, batch=batch, seq=seq, tt=512)
        xp = _mlp(_merge(xp, (o_a, o_b, o_c, o_d), lw, tm=tm_p), lw, gf, tm=tm_p, final=final)
        new_p = (ka.reshape(batch, seq, N_KV, HEAD_DIM), va.reshape(batch, seq, N_KV, HEAD_DIM),
                 kb.reshape(batch, seq, N_KV, HEAD_DIM), vb.reshape(batch, seq, N_KV, HEAD_DIM),
                 ckv.reshape(batch, seq, MLA_KV_RANK), kpe.reshape(batch, seq, MLA_ROPE),
                 h_last.reshape(batch, LRU_WIDTH),
                 xrg.reshape(batch, seq, 512)[:, seq - (CONV_W - 1):, :LRU_WIDTH])
        for lst, a in zip(p_states, new_p):
            lst.append(a)

        (qa, qb, qbl, ka, va, kb, vb, _, _, _, _, qc, qpe, _, ckv, kpe, xrg) = _proj(
            xs, tab_s, lw, tm=tm_s, prompt=False)
        qa16 = _rows16(qa.reshape(n_dec, N_HEADS, LANES))
        qb16 = _rows16(qb.reshape(n_dec, N_HEADS, LANES))
        qbl16 = _rows16(qbl.reshape(n_dec, N_HEADS, LANES))
        ql16 = _rows16(qc.reshape(n_dec, N_HEADS, 256)[:, :, :LANES])
        qp16 = _rows16(qpe.reshape(n_dec, N_HEADS, MLA_ROPE))
        o_a = _pick_head_lanes(_sb_sample(qa16, c_sb_k, c_sb_v, pt_flat, consts, layer=l, n_pages=n_pages,
                                          cp=SB_CHUNK_PAGES))
        o_b = _pick_head_lanes(_moba_sample(qb16, qbl16, kb[:, None, :], vb[:, None, :], c_mb_k, c_mb_v, pt_flat,
                                            layer=l, n_pages=n_pages))
        o_c16 = _mla_sample(ql16, qp16, ckv[:, None, :], kpe[:, None, :], lw["wuv_flat"], cache_mla_ckv,
                            c_kpe, pt_flat, layer=l, n_pages=n_pages)
        o_c = jnp.concatenate([o_c16[:, h, h * MLA_V:(h + 1) * MLA_V] for h in range(N_HEADS)], axis=1)
        buf0 = state_lru_conv[l]
        o_d, h_new = _lru_sample(xrg, jnp.transpose(buf0, (1, 0, 2)), state_lru_h[l], lw)
        outs = (o_a.astype(BF16), o_b.astype(BF16), o_c.astype(BF16), o_d)
        xs = _mlp(_merge(xs, outs, lw, tm=tm_s), lw, gf, tm=tm_s, final=final)
        new_s = (ka.reshape(n_dec, 1, N_KV, HEAD_DIM), va.reshape(n_dec, 1, N_KV, HEAD_DIM),
                 kb.reshape(n_dec, 1, N_KV, HEAD_DIM), vb.reshape(n_dec, 1, N_KV, HEAD_DIM),
                 ckv.reshape(n_dec, 1, MLA_KV_RANK), kpe.reshape(n_dec, 1, MLA_ROPE), h_new,
                 jnp.concatenate([buf0[:, 1:], xrg[:, None, :LRU_WIDTH]], axis=1))
        for lst, a in zip(s_states, new_s):
            lst.append(a)

    y_prompt = xp.reshape(batch, seq, D_MODEL)
    y_sample = xs.reshape(n_dec, 1, D_MODEL)
    return (y_prompt, y_sample, *[jnp.stack(s) for s in p_states], *[jnp.stack(s) for s in s_states])
```

```python
import functools

import jax
import jax.numpy as jnp
from jax import lax
from jax.experimental import pallas as pl
from jax.experimental.pallas import tpu as pltpu

F32 = jnp.float32
BF16 = jnp.bfloat16

D_MODEL = 1024
HEAD_DIM = 64
N_HEADS = 4
N_KV = 2
MB_BLOCK = 256
MB_TOPK = 3
MLA_Q_RANK = 192
MLA_KV_RANK = 128
MLA_NOPE = 64
MLA_ROPE = 32
MLA_V = 64
ROPE_THETA = 10000.0
LRU_WIDTH = 256
LRU_BLOCKS = 4
CONV_W = 4
LRU_C = 8.0
N_BRANCH = 4
BRANCH_W = 256
D_FF = 4 * D_MODEL
EPS = 1e-6
PAGE = 128

LANES = 128
ROWS16 = 16
NEG_BIG = -1e30
SB_DEAD = -104.0
SB_CHUNK_PAGES = 4
SB_TQ, SB_TK = 256, 256
MLA_TQ, MLA_TK = 128, 256
VMEM_LIMIT = 56 * 1024 * 1024

NT = (((1,), (1,)), ((), ()))


def _dot(a, b):
    return jnp.dot(a, b, preferred_element_type=F32)


def _dot_nt(a, b):
    return lax.dot_general(a, b, NT, preferred_element_type=F32)


def _split2(x):
    hi = x.astype(BF16)
    lo = (x - hi.astype(F32)).astype(BF16)
    return hi, lo


def _split3(x):
    hi = x.astype(BF16)
    r = x - hi.astype(F32)
    mid = r.astype(BF16)
    lo = (r - mid.astype(F32)).astype(BF16)
    return hi, mid, lo


def _rms(x, g, n):
    ms = jnp.sum(x * x, axis=-1, keepdims=True) * (1.0 / n)
    return x * lax.rsqrt(ms + EPS) * g


def _softplus_neg_abs(z):
    return jnp.log(1.0 + jnp.exp(-jnp.abs(z)))


def _gelu(x):
    return 0.5 * x * (1.0 + jnp.tanh(0.7978845608028654 * (x + 0.044715 * x * x * x)))


def _sigmoid(x):
    return 1.0 / (1.0 + jnp.exp(-x))


def _cparams(sem):
    return pltpu.CompilerParams(dimension_semantics=sem, vmem_limit_bytes=VMEM_LIMIT)


def _const_spec(shape):
    nd = len(shape)
    return pl.BlockSpec(shape, lambda *_: (0,) * nd)


def _proj_body(x_ref, g1_ref, w_ref, wlo_ref, qg_ref, kvg_ref, wuqn_ref, wuqp_ref, wuk_ref, cos_ref, sin_ref,
               qa_ref, qb_ref, qbl_ref, ka_ref, va_ref, kb_ref, vb_ref, kab_ref, vab_ref, kbb_ref, vbb_ref,
               qc_ref, qpe_ref, kc_ref, ckv_ref, kpe_ref, xrg_ref, *prompt_refs, tm, prompt):
    x = x_ref[...]
    u32 = _rms(x, g1_ref[...], D_MODEL)
    u, u_lo = _split2(u32)
    acc = _dot(u, w_ref[...])
    fine = _dot(u_lo, w_ref[:, 512:896]) + _dot(u, wlo_ref[...])
    lane = lax.broadcasted_iota(jnp.int32, (tm, LANES), 1)
    lo = lane < HEAD_DIM

    def pad_heads(q):
        v0, v1 = q[:, :LANES], q[:, LANES:]
        r0, r1 = pltpu.roll(v0, HEAD_DIM, 1), pltpu.roll(v1, HEAD_DIM, 1)
        z = jnp.zeros_like(v0)
        heads = [jnp.where(lo, v0, z), jnp.where(lo, r0, z), jnp.where(lo, z, r1), jnp.where(lo, z, v1)]
        return jnp.concatenate(heads, axis=1) * (HEAD_DIM ** -0.5)

    qa_ref[...] = pad_heads(acc[:, 0:256]).astype(BF16)
    ka, va = acc[:, 256:384], acc[:, 384:512]
    qb_pad = pad_heads(acc[:, 512:768] + fine[:, :256])
    qb_hi, qb_lo = _split2(qb_pad)
    qb_ref[...] = qb_hi
    qbl_ref[...] = qb_lo
    kb, vb = acc[:, 768:896] + fine[:, 256:], acc[:, 896:1024]
    ka_ref[...] = ka
    va_ref[...] = va
    kb_ref[...] = kb
    vb_ref[...] = vb
    kab_ref[...] = ka.astype(BF16)
    vab_ref[...] = va.astype(BF16)
    kbb_ref[...] = kb.astype(BF16)
    vbb_ref[...] = vb.astype(BF16)
    if prompt:
        kmean_ref, qct_ref, vt_ref, qbt_ref, qbtl_ref, vbt_ref = prompt_refs
        kmean_ref[...] = jnp.sum(kb.reshape(tm // MB_BLOCK, MB_BLOCK, LANES), axis=1) * (1.0 / MB_BLOCK)
        qbt_hi, qbt_lo = _split2(jnp.transpose(qb_pad))
        qbt_ref[...] = qbt_hi
        qbtl_ref[...] = qbt_lo
        vb_t = jnp.transpose(vb).astype(BF16)
        for j in range(tm // MB_BLOCK):
            vbt_ref[j] = vb_t[:, j * MB_BLOCK:(j + 1) * MB_BLOCK]

    cqn = _rms(acc[:, 1024:1280], qg_ref[...], MLA_Q_RANK).astype(BF16)
    q_nope = _dot(cqn, wuqn_ref[...]).astype(BF16)
    q_pe = _dot(cqn, wuqp_ref[...])
    q_lat = _dot(q_nope, wuk_ref[...])
    cos, sin = cos_ref[...], sin_ref[...]
    first_half = (lane % MLA_ROPE) < (MLA_ROPE // 2)

    def rope(v):
        swapped = jnp.where(first_half, pltpu.roll(v, LANES - MLA_ROPE // 2, 1), pltpu.roll(v, MLA_ROPE // 2, 1))
        return v * cos + swapped * sin

    scale = (MLA_NOPE + MLA_ROPE) ** -0.5
    q_pe = rope(q_pe) * scale
    qpe_ref[...] = q_pe.astype(BF16)
    pieces = []
    for h in range(N_HEADS):
        pieces.append(q_lat[:, h * LANES:(h + 1) * LANES] * scale)
        pieces.append(jnp.where(lane // MLA_ROPE == h, q_pe, 0.0))
    qc = jnp.concatenate(pieces, axis=1)
    qc_ref[...] = qc.astype(BF16)

    ckvn = _rms(acc[:, 1280:1408], kvg_ref[...], MLA_KV_RANK)
    kpe4 = rope(acc[:, 1408:1536])
    if prompt:
        qct_ref[...] = jnp.transpose(qc).astype(BF16)
        ckv_t = jnp.transpose(ckvn).astype(BF16)
        for j in range(tm // MLA_TK):
            vt_ref[j] = ckv_t[:, j * MLA_TK:(j + 1) * MLA_TK]
    ckv_ref[...] = ckvn
    kpe_ref[...] = kpe4[:, :MLA_ROPE]
    kc_ref[...] = jnp.concatenate([ckvn, kpe4], axis=1).astype(BF16)
    xrg_ref[...] = acc[:, 1536:2048]


def _proj(x, tables, lw, *, tm, prompt):
    n = x.shape[0]
    nt = n // tm
    cos_t, sin_t = tables
    npos = cos_t.shape[0] // tm
    row = lambda w: pl.BlockSpec((tm, w), lambda i: (i, 0))
    tab = pl.BlockSpec((tm, LANES), lambda i: (i % npos, 0))
    out_shape = [
        jax.ShapeDtypeStruct((n, 512), BF16), jax.ShapeDtypeStruct((n, 512), BF16),
        jax.ShapeDtypeStruct((n, 512), BF16),
        jax.ShapeDtypeStruct((n, LANES), F32), jax.ShapeDtypeStruct((n, LANES), F32),
        jax.ShapeDtypeStruct((n, LANES), F32), jax.ShapeDtypeStruct((n, LANES), F32),
        jax.ShapeDtypeStruct((n, LANES), BF16), jax.ShapeDtypeStruct((n, LANES), BF16),
        jax.ShapeDtypeStruct((n, LANES), BF16), jax.ShapeDtypeStruct((n, LANES), BF16),
        jax.ShapeDtypeStruct((n, 1024), BF16), jax.ShapeDtypeStruct((n, LANES), BF16),
        jax.ShapeDtypeStruct((n, 256), BF16), jax.ShapeDtypeStruct((n, LANES), F32),
        jax.ShapeDtypeStruct((n, MLA_ROPE), F32), jax.ShapeDtypeStruct((n, 512), F32),
    ]
    out_specs = [row(512), row(512), row(512), row(LANES), row(LANES), row(LANES), row(LANES),
                 row(LANES), row(LANES), row(LANES), row(LANES),
                 row(1024), row(LANES), row(256), row(LANES), row(MLA_ROPE), row(512)]
    if prompt:
        nb = tm // MB_BLOCK
        out_shape += [jax.ShapeDtypeStruct((nt, nb, LANES), F32),
                      jax.ShapeDtypeStruct((N_HEADS * 256, n), BF16),
                      jax.ShapeDtypeStruct((n // MLA_TK, MLA_KV_RANK, MLA_TK), BF16),
                      jax.ShapeDtypeStruct((N_HEADS * LANES, n), BF16),
                      jax.ShapeDtypeStruct((N_HEADS * LANES, n), BF16),
                      jax.ShapeDtypeStruct((n // MB_BLOCK, LANES, MB_BLOCK), BF16)]
        out_specs += [pl.BlockSpec((None, nb, LANES), lambda i: (i, 0, 0)),
                      pl.BlockSpec((N_HEADS * 256, tm), lambda i: (0, i)),
                      pl.BlockSpec((tm // MLA_TK, MLA_KV_RANK, MLA_TK), lambda i: (i, 0, 0)),
                      pl.BlockSpec((N_HEADS * LANES, tm), lambda i: (0, i)),
                      pl.BlockSpec((N_HEADS * LANES, tm), lambda i: (0, i)),
                      pl.BlockSpec((nb, LANES, MB_BLOCK), lambda i: (i, 0, 0))]
    return pl.pallas_call(
        functools.partial(_proj_body, tm=tm, prompt=prompt),
        grid=(nt,),
        in_specs=[row(D_MODEL), _const_spec((1, D_MODEL)), _const_spec((D_MODEL, 2048)),
                  _const_spec((D_MODEL, 384)),
                  _const_spec((1, 256)), _const_spec((1, LANES)), _const_spec((256, 256)),
                  _const_spec((256, LANES)), _const_spec((256, 512)), tab, tab],
        out_specs=out_specs, out_shape=out_shape,
        compiler_params=_cparams(("parallel",)), name="proj",
    )(x, lw["g1"], lw["w_proj"], lw["w_proj_lo"], lw["qg"], lw["kvg"], lw["wuq_nope"], lw["wuq_pe"], lw["wuk_bd"],
      cos_t, sin_t)


def _head_lanes_out(acc, h, tq):
    a0, a1 = acc[:tq], acc[tq:]
    r0, r1 = pltpu.roll(a0, HEAD_DIM, 1), pltpu.roll(a1, HEAD_DIM, 1)
    lo = lax.broadcasted_iota(jnp.int32, (tq, LANES), 1) < HEAD_DIM
    first = jnp.where(h == 0, a0, r0)
    second = jnp.where(h == 0, r1, a1)
    return jnp.where(lo, first, second)


def _sb_prompt_body(q_ref, k_ref, v_ref, o_ref, *, tq, tk):
    h = pl.program_id(1)
    qi = pl.program_id(2)
    qblk = q_ref[...]
    q = jnp.concatenate([qblk[:, :LANES], qblk[:, LANES:]], axis=0)
    rows = 2 * tq
    t_pos = qi * tq + lax.broadcasted_iota(jnp.int32, (rows, tk), 0) % tq
    s_loc = lax.broadcasted_iota(jnp.int32, (rows, tk), 1)
    suffix = (lax.broadcasted_iota(jnp.int32, (tk, tk), 0) > lax.broadcasted_iota(jnp.int32, (tk, tk), 1)
              ).astype(BF16)

    def block(kb, carry, acc, masked):
        start = pl.multiple_of(kb * tk, tk)
        z = _dot_nt(q, k_ref[pl.ds(start, tk), :])
        sp = _softplus_neg_abs(z)
        log_keep = jnp.minimum(-z, 0.0) - sp
        log_beta = jnp.minimum(z, 0.0) - sp
        if masked:
            causal = (start + s_loc) < t_pos
            log_keep = jnp.where(causal, log_keep, 0.0)
        hi, lo = _split2(log_keep)
        later = _dot(hi, suffix) + _dot(lo, suffix)
        w = jnp.exp(log_beta + later + carry)
        if masked:
            w = jnp.where(causal, w, 0.0)
        acc = acc + _dot(w.astype(BF16), v_ref[pl.ds(start, tk), :])
        carry = carry + later[:, :1] + log_keep[:, :1]
        return carry, acc

    carry = jnp.zeros((rows, 1), F32)
    acc = jnp.zeros((rows, LANES), F32)
    n_diag = tq // tk
    for d in range(n_diag):
        carry, acc = block(qi * n_diag + (n_diag - 1 - d), carry, acc, True)

    def cond(c):
        kb, carry, _ = c
        return jnp.logical_and(kb >= 0, jnp.max(carry) > SB_DEAD)

    def step(c):
        kb, carry, acc = c
        carry, acc = block(kb, carry, acc, False)
        return kb - 1, carry, acc

    _, _, acc = lax.while_loop(cond, step, (qi * n_diag - 1, carry, acc))
    o_ref[...] = _head_lanes_out(acc, h, tq).astype(o_ref.dtype)


def _sb_prompt(qa_pad, kab, vab, *, batch, seq, tq, tk):
    nq = seq // tq
    return pl.pallas_call(
        functools.partial(_sb_prompt_body, tq=tq, tk=tk),
        grid=(batch, N_KV, nq),
        in_specs=[pl.BlockSpec((tq, 256), lambda b, h, i: (b * nq + i, h)),
                  pl.BlockSpec((seq, LANES), lambda b, h, i: (b, 0)),
                  pl.BlockSpec((seq, LANES), lambda b, h, i: (b, 0))],
        out_specs=pl.BlockSpec((tq, LANES), lambda b, h, i: (b * nq + i, h)),
        out_shape=jax.ShapeDtypeStruct((batch * seq, 256), BF16),
        compiler_params=_cparams(("parallel", "parallel", "arbitrary")), name="sb_prompt",
    )(qa_pad, kab, vab)


def _top3(gate, valid, axis=1):
    nb = gate.shape[axis]
    idx = lax.broadcasted_iota(jnp.int32, gate.shape, axis).astype(F32)
    g = jnp.where(valid, gate, -jnp.inf)
    sel = jnp.zeros(gate.shape, jnp.bool_)
    picks, oks = [], []
    for _ in range(MB_TOPK):
        m = jnp.max(g, axis=axis, keepdims=True)
        first = jnp.min(jnp.where(g == m, idx, float(nb)), axis=axis, keepdims=True)
        pick = idx == first
        sel = jnp.logical_or(sel, pick)
        g = jnp.where(pick, -jnp.inf, g)
        picks.append(first)
        oks.append(m > -jnp.inf)
    return jnp.logical_and(sel, valid), picks, oks


def _top3_mask(gate, valid):
    return _top3(gate, valid)[0]


def _alibi_slope(head):
    return jnp.exp2(-2.0 * (head + 1).astype(F32))


def _moba_prompt_body(qt_ref, qtl_ref, k_ref, vt_ref, km_ref, o_ref, sel_ref, *, tq, nb_pad):
    h = pl.program_id(1)
    qi = pl.program_id(2)
    tk = MB_BLOCK
    groups = N_HEADS // N_KV
    cols = groups * tq
    own = (qi * tq) // tk
    q_t = jnp.concatenate([qt_ref[g * LANES:(g + 1) * LANES, :] for g in range(groups)], axis=1)
    q_t_lo = jnp.concatenate([qtl_ref[g * LANES:(g + 1) * LANES, :] for g in range(groups)], axis=1)
    col = lax.broadcasted_iota(jnp.int32, (1, cols), 1)
    t_off = qi * tq - own * tk + col % tq
    slope = _alibi_slope(2 * h + col // tq)
    s_loc = lax.broadcasted_iota(jnp.int32, (tk, cols), 0)
    key_off = lax.broadcasted_iota(jnp.int32, (tk, 1), 0).astype(F32)
    blk = lax.broadcasted_iota(jnp.int32, (nb_pad, cols), 0)

    km_hi, km_lo = _split2(km_ref[...])
    gate = (_dot(km_hi, q_t) + _dot(km_hi, q_t_lo) + _dot(km_lo, q_t)) * (HEAD_DIM ** 0.5)
    sel_ref[...] = _top3(gate, blk < own, axis=0)[0].astype(F32)
    row_bias = slope * t_off.astype(F32)

    def scores(n):
        start = pl.multiple_of(n * tk, tk)
        return _dot(k_ref[pl.ds(start, tk), :], q_t)

    def process(s, n, c, own_block):
        m, l, acc = c
        s = s + key_off * slope
        if own_block:
            s = jnp.where(s_loc <= t_off, s, NEG_BIG)
            shift = row_bias
        else:
            chosen = sel_ref[pl.ds(n, 1), :] > 0.5
            gap = jnp.full((1, 1), (own - n) * tk, jnp.int32).astype(F32)
            shift = row_bias + slope * gap + jnp.where(chosen, 0.0, -NEG_BIG)
        m_new = jnp.maximum(m, jnp.max(s, axis=0, keepdims=True) - shift)
        alpha = jnp.exp(m - m_new)
        p = jnp.exp(s - (shift + m_new))
        l = alpha * l + jnp.sum(p, axis=0, keepdims=True)
        acc = alpha * acc + _dot(vt_ref[n], p.astype(BF16))
        return m_new, l, acc

    init = (jnp.full((1, cols), NEG_BIG, F32), jnp.zeros((1, cols), F32), jnp.zeros((LANES, cols), F32))
    state = process(scores(own), own, init, True)

    def body(n, c):
        s_cur, state = c
        s_next = scores(jnp.minimum(n + 1, own))
        return s_next, process(s_cur, n, state, False)

    _, (m, l, acc) = lax.fori_loop(0, own, body, (scores(0), state))
    o_t = acc / l
    mine = jnp.where(h == 0, o_t[:HEAD_DIM], o_t[HEAD_DIM:])
    o_gd = jnp.concatenate([mine[:, g * tq:(g + 1) * tq] for g in range(groups)], axis=0)
    o_ref[...] = jnp.transpose(o_gd).astype(o_ref.dtype)


def _moba_prompt(qb_t, qb_t_lo, kbb, vb_t, kmean, *, batch, seq, tq):
    nq = seq // tq
    nkb = seq // MB_BLOCK
    nb_pad = kmean.shape[1]
    groups = N_HEADS // N_KV
    q_spec = pl.BlockSpec((groups * LANES, tq), lambda b, h, i: (h, b * nq + i))
    return pl.pallas_call(
        functools.partial(_moba_prompt_body, tq=tq, nb_pad=nb_pad),
        grid=(batch, N_KV, nq),
        in_specs=[q_spec, q_spec,
                  pl.BlockSpec((seq, LANES), lambda b, h, i: (b, 0)),
                  pl.BlockSpec((nkb, LANES, MB_BLOCK), lambda b, h, i: (b, 0, 0)),
                  pl.BlockSpec((None, nb_pad, LANES), lambda b, h, i: (b, 0, 0))],
        out_specs=pl.BlockSpec((tq, LANES), lambda b, h, i: (b * nq + i, h)),
        out_shape=jax.ShapeDtypeStruct((batch * seq, 256), BF16),
        scratch_shapes=[pltpu.VMEM((nb_pad, groups * tq), F32)],
        compiler_params=_cparams(("parallel", "parallel", "arbitrary")), name="moba_prompt",
    )(qb_t, qb_t_lo, kbb, vb_t, kmean)


def _mla_prompt_body(qt_ref, kc_ref, vt_ref, wuv_ref, o_ref, *, tq, tk):
    qi = pl.program_id(1)
    cols = N_HEADS * tq
    q_t = jnp.concatenate([qt_ref[h * 256:(h + 1) * 256, :] for h in range(N_HEADS)], axis=1)
    s_loc = lax.broadcasted_iota(jnp.int32, (tk, cols), 0)
    t_loc = lax.broadcasted_iota(jnp.int32, (tk, cols), 1) % tq
    diag = (qi * tq) // tk

    def scores(kb):
        start = pl.multiple_of(kb * tk, tk)
        return _dot(kc_ref[pl.ds(start, tk), :], q_t)

    def process(s, kb, c, masked):
        m, l, acc = c
        if masked:
            s = jnp.where(kb * tk + s_loc <= qi * tq + t_loc, s, NEG_BIG)
        m_new = jnp.maximum(m, jnp.max(s, axis=0, keepdims=True))
        alpha = jnp.exp(m - m_new)
        p = jnp.exp(s - m_new)
        l = alpha * l + jnp.sum(p, axis=0, keepdims=True)
        acc = alpha * acc + _dot(vt_ref[kb], p.astype(BF16))
        return m_new, l, acc

    def body(i, c):
        s_cur, state = c
        s_next = scores(i + 1)
        return s_next, process(s_cur, i, state, False)

    init = (jnp.full((1, cols), NEG_BIG, F32), jnp.zeros((1, cols), F32), jnp.zeros((MLA_KV_RANK, cols), F32))
    s_cur, state = lax.fori_loop(0, diag, body, (scores(0), init))
    m, l, acc = process(s_cur, diag, state, True)
    o_lat_t = (acc / l).astype(BF16)
    contract_rows = (((0,), (0,)), ((), ()))
    outs = [lax.dot_general(o_lat_t[:, h * tq:(h + 1) * tq], wuv_ref[h], contract_rows, preferred_element_type=F32)
            for h in range(N_HEADS)]
    o_ref[...] = jnp.concatenate(outs, axis=1).astype(o_ref.dtype)


def _mla_prompt(qc_t, kc, v_t, wuv, *, batch, seq, tq, tk):
    nq = seq // tq
    nkb = seq // tk
    return pl.pallas_call(
        functools.partial(_mla_prompt_body, tq=tq, tk=tk),
        grid=(batch, nq),
        in_specs=[pl.BlockSpec((N_HEADS * 256, tq), lambda b, i: (0, b * nq + i)),
                  pl.BlockSpec((seq, 256), lambda b, i: (b, 0)),
                  pl.BlockSpec((nkb, MLA_KV_RANK, tk), lambda b, i: (b, 0, 0)),
                  _const_spec((N_HEADS, MLA_KV_RANK, MLA_V))],
        out_specs=pl.BlockSpec((tq, 256), lambda b, i: (b * nq + i, 0)),
        out_shape=jax.ShapeDtypeStruct((batch * seq, 256), BF16),
        compiler_params=_cparams(("parallel", "arbitrary")), name="mla_prompt",
    )(qc_t, kc, v_t, wuv)


def _lru_gates(xc, wa_ref, ba_ref, wx_ref, bx_ref, lam_ref):
    xb = xc.astype(BF16)
    r = _sigmoid(_dot(xb, wa_ref[...]) + ba_ref[...])
    i = _sigmoid(_dot(xb, wx_ref[...]) + bx_ref[...])
    lam = lam_ref[...]
    softplus_neg_lam = jnp.maximum(-lam, 0.0) + _softplus_neg_abs(lam)
    log_a = -LRU_C * r * softplus_neg_lam
    a = jnp.exp(log_a)
    th = jnp.tanh(log_a)
    u = jnp.sqrt(-2.0 * th / (1.0 - th)) * i * xc
    return a, u


def _lru_prompt_body(xrg_ref, cw_ref, cb_ref, wa_ref, ba_ref, wx_ref, bx_ref, lam_ref,
                     od_ref, hlast_ref, xpad, hprev, *, tt):
    t = pl.program_id(1)

    @pl.when(t == 0)
    def _():
        xpad[0:8, :] = jnp.zeros((8, LRU_WIDTH), F32)
        hprev[...] = jnp.zeros((1, LRU_WIDTH), F32)

    x = xrg_ref[:, :LRU_WIDTH]
    xg = xrg_ref[:, LRU_WIDTH:]
    xpad[8:8 + tt, :] = x
    xc = cb_ref[...] + cw_ref[CONV_W - 1:CONV_W, :] * x
    for i in range(CONV_W - 1):
        back = CONV_W - 1 - i
        xc = xc + cw_ref[i:i + 1, :] * xpad[8 - back:8 - back + tt, :]
    xpad[0:8, :] = x[tt - 8:, :]

    a, u = _lru_gates(xc, wa_ref, ba_ref, wx_ref, bx_ref, lam_ref)
    row = lax.broadcasted_iota(jnp.int32, (tt, LRU_WIDTH), 0)
    d = 1
    while d < tt:
        keep = row >= d
        a_sh = jnp.where(keep, pltpu.roll(a, d, 0), 1.0)
        u_sh = jnp.where(keep, pltpu.roll(u, d, 0), 0.0)
        u = u + a * u_sh
        a = a * a_sh
        d *= 2
    hs = a * hprev[...] + u
    hprev[...] = hs[tt - 1:, :]
    hlast_ref[...] = hs[tt - 1:, :]
    od_ref[...] = (hs * _gelu(xg)).astype(od_ref.dtype)


def _lru_prompt(xrg, lw, *, batch, seq, tt):
    nt = seq // tt
    vec = _const_spec((1, LRU_WIDTH))
    mat = _const_spec((LRU_WIDTH, LRU_WIDTH))
    return pl.pallas_call(
        functools.partial(_lru_prompt_body, tt=tt),
        grid=(batch, nt),
        in_specs=[pl.BlockSpec((tt, 512), lambda b, t: (b * nt + t, 0)),
                  _const_spec((CONV_W, LRU_WIDTH)), vec, mat, vec, mat, vec, vec],
        out_specs=[pl.BlockSpec((tt, LRU_WIDTH), lambda b, t: (b * nt + t, 0)),
                   pl.BlockSpec((None, 1, LRU_WIDTH), lambda b, t: (b, 0, 0))],
        out_shape=[jax.ShapeDtypeStruct((batch * seq, LRU_WIDTH), BF16),
                   jax.ShapeDtypeStruct((batch, 1, LRU_WIDTH), F32)],
        scratch_shapes=[pltpu.VMEM((tt + 8, LRU_WIDTH), F32), pltpu.VMEM((1, LRU_WIDTH), F32)],
        compiler_params=_cparams(("parallel", "arbitrary")), name="lru_prompt",
    )(xrg, lw["conv_w"], lw["conv_b"], lw["w_rg_a"], lw["b_rg_a"], lw["w_rg_x"], lw["b_rg_x"], lw["lam"])


def _lru_sample_body(xrg_ref, buf_ref, h0_ref, cw_ref, cb_ref, wa_ref, ba_ref, wx_ref, bx_ref, lam_ref,
                     od_ref, h_ref):
    x = xrg_ref[:, :LRU_WIDTH]
    xg = xrg_ref[:, LRU_WIDTH:]
    xc = cb_ref[...] + cw_ref[CONV_W - 1:CONV_W, :] * x
    for i in range(CONV_W - 1):
        xc = xc + cw_ref[i:i + 1, :] * buf_ref[i]
    a, u = _lru_gates(xc, wa_ref, ba_ref, wx_ref, bx_ref, lam_ref)
    h = a * h0_ref[...] + u
    h_ref[...] = h
    od_ref[...] = (h * _gelu(xg)).astype(od_ref.dtype)


def _lru_sample(xrg, buf_t, h0, lw):
    s = xrg.shape[0]
    vec = _const_spec((1, LRU_WIDTH))
    mat = _const_spec((LRU_WIDTH, LRU_WIDTH))
    return pl.pallas_call(
        _lru_sample_body,
        grid=(1,),
        in_specs=[_const_spec((s, 512)), _const_spec((CONV_W - 1, s, LRU_WIDTH)), _const_spec((s, LRU_WIDTH)),
                  _const_spec((CONV_W, LRU_WIDTH)), vec, mat, vec, mat, vec, vec],
        out_specs=[_const_spec((s, LRU_WIDTH)), _const_spec((s, LRU_WIDTH))],
        out_shape=[jax.ShapeDtypeStruct((s, LRU_WIDTH), BF16), jax.ShapeDtypeStruct((s, LRU_WIDTH), F32)],
        compiler_params=_cparams(("arbitrary",)), name="lru_sample",
    )(xrg, buf_t, h0, lw["conv_w"], lw["conv_b"], lw["w_rg_a"], lw["b_rg_a"], lw["w_rg_x"], lw["b_rg_x"],
      lw["lam"])


def _merge_body(x_ref, oa_ref, ob_ref, oc_ref, od_ref, g1_ref, wgl_ref, bg_ref, wbr_ref, wout_ref, y_ref):
    x = x_ref[...]
    u = _rms(x, g1_ref[...], D_MODEL).astype(BF16)
    s = jnp.zeros(x.shape, F32)
    for k, o_ref in enumerate((oa_ref, ob_ref, oc_ref, od_ref)):
        cols = slice(k * D_MODEL, (k + 1) * D_MODEL)
        gate = _sigmoid(_dot(u, wgl_ref[:, cols]) + bg_ref[:, cols])
        s = s + gate * _dot(o_ref[...], wbr_ref[k])
    y_ref[...] = x + _dot(s.astype(BF16), wout_ref[...])


def _merge(x, outs, lw, *, tm):
    n = x.shape[0]
    row = lambda w: pl.BlockSpec((tm, w), lambda i: (i, 0))
    return pl.pallas_call(
        _merge_body,
        grid=(n // tm,),
        in_specs=[row(D_MODEL), row(256), row(256), row(256), row(256), _const_spec((1, D_MODEL)),
                  _const_spec((D_MODEL, N_BRANCH * D_MODEL)), _const_spec((1, N_BRANCH * D_MODEL)),
                  _const_spec((N_BRANCH, BRANCH_W, D_MODEL)), _const_spec((D_MODEL, D_MODEL))],
        out_specs=row(D_MODEL),
        out_shape=jax.ShapeDtypeStruct((n, D_MODEL), F32),
        compiler_params=_cparams(("parallel",)), name="merge",
    )(x, *outs, lw["g1"], lw["w_gl"], lw["b_gate"], lw["w_branch"], lw["w_out"])


def _mlp_body(x_ref, g2_ref, w1_ref, w2_ref, gf_ref, y_ref, *, final, chunk):
    x = x_ref[...]
    n = _rms(x, g2_ref[...], D_MODEL).astype(BF16)
    acc = x
    for c in range(D_FF // chunk):
        cols = slice(c * chunk, (c + 1) * chunk)
        h = jnp.maximum(_dot(n, w1_ref[:, cols]), 0.0)
        acc = acc + _dot((h * h).astype(BF16), w2_ref[cols, :])
    if final:
        acc = _rms(acc, gf_ref[...], D_MODEL)
    y_ref[...] = acc


def _mlp(x, lw, gf, *, tm, final):
    n = x.shape[0]
    row = pl.BlockSpec((tm, D_MODEL), lambda i: (i, 0))
    return pl.pallas_call(
        functools.partial(_mlp_body, final=final, chunk=1024),
        grid=(n // tm,),
        in_specs=[row, _const_spec((1, D_MODEL)), _const_spec((D_MODEL, D_FF)), _const_spec((D_FF, D_MODEL)),
                  _const_spec((1, D_MODEL))],
        out_specs=row,
        out_shape=jax.ShapeDtypeStruct((n, D_MODEL), F32),
        compiler_params=_cparams(("parallel",)), name="mlp",
    )(x, lw["g2"], lw["w_mlp1"], lw["w_mlp2"], gf)


def _page_copy(cache, layer, pt_ref, s, p, n_pages, buf, slot, sem, layout):
    page = pt_ref[s * n_pages + p]
    if layout == "lanes":
        dst = buf.at[slot, :, pl.ds(pl.multiple_of(p * PAGE, PAGE), PAGE)]
    elif layout == "blocks":
        per = MB_BLOCK // PAGE
        dst = buf.at[slot, p // per, :, pl.ds(pl.multiple_of((p % per) * PAGE, PAGE), PAGE)]
    else:
        dst = buf.at[slot, pl.ds(pl.multiple_of(p * PAGE, PAGE), PAGE)]
    return pltpu.make_async_copy(cache.at[layer, page], dst, sem.at[slot])


def _paged_prefetch(caches, bufs, sems, layouts, pt_ref, layer, n_pages):
    s = pl.program_id(0)
    ns = pl.num_programs(0)
    streams = tuple(zip(caches, bufs, sems, layouts))

    def copies(sample, slot):
        return [_page_copy(cache, layer, pt_ref, sample, p, n_pages, buf, slot, sem, kl)
                for p in range(n_pages) for cache, buf, sem, kl in streams]

    @pl.when(s == 0)
    def _():
        for c in copies(0, 0):
            c.start()

    slot = s % 2
    nxt, nxt_slot = jnp.minimum(s + 1, ns - 1), (s + 1) % 2
    for c in copies(s, slot):
        c.wait()
    for c in copies(nxt, nxt_slot):
        c.start()

    def drain():
        @pl.when(s == ns - 1)
        def _():
            for c in copies(nxt, nxt_slot):
                c.wait()

    return slot, drain


def _to_blocks(x, nblk):
    return jnp.concatenate([x[:, b * LANES:(b + 1) * LANES] for b in range(nblk)], axis=0)


def _from_blocks(x, nblk):
    r = x.shape[0] // nblk
    return jnp.concatenate([x[b * r:(b + 1) * r] for b in range(nblk)], axis=1)


def _sb_sample_body(pt_ref, q_ref, suffix_ref, kc_hbm, vc_hbm, o_ref, kbuf, vbuf, kmore, vmore, sems, more_sems,
                    *, layer, n_pages, cp):
    s = pl.program_id(0)
    ns = pl.num_programs(0)
    n_chunks = n_pages // cp

    def chunk_copies(sample, j, kdst, vdst, ksem, vsem):
        copies = []
        for i in range(cp):
            page = pt_ref[sample * n_pages + n_pages - (j + 1) * cp + i]
            lanes = pl.ds(i * PAGE, PAGE)
            copies.append(pltpu.make_async_copy(kc_hbm.at[layer, page], kdst.at[:, lanes], ksem))
            copies.append(pltpu.make_async_copy(vc_hbm.at[layer, page], vdst.at[:, lanes], vsem))
        return copies

    def first_chunk(sample, slot):
        return chunk_copies(sample, 0, kbuf.at[slot], vbuf.at[slot], sems.at[0, slot], sems.at[1, slot])

    @pl.when(s == 0)
    def _():
        for c in first_chunk(0, 0):
            c.start()

    @pl.when(s + 1 < ns)
    def _():
        for c in first_chunk(s + 1, (s + 1) % 2):
            c.start()

    slot = s % 2
    for c in first_chunk(s, slot):
        c.wait()

    q = q_ref[...]
    suffix = suffix_ref[...]
    r = lax.broadcasted_iota(jnp.int32, (8 * cp, 8 * cp), 0)
    c_ = lax.broadcasted_iota(jnp.int32, (8 * cp, 8 * cp), 1)
    later_pages = jnp.logical_and(c_ % 8 == r % 8, c_ > r).astype(BF16)

    def process(k_t, v_t, carry, acc):
        z = _dot(q, k_t.astype(BF16))[:8]
        zb = _to_blocks(z, cp)
        sp = _softplus_neg_abs(zb)
        log_keep = jnp.minimum(-zb, 0.0) - sp
        log_beta = jnp.minimum(zb, 0.0) - sp
        hi, lo = _split2(log_keep)
        later = _dot(hi, suffix) + _dot(lo, suffix)
        total = jnp.broadcast_to(later[:, :1] + log_keep[:, :1], later.shape)
        t_hi, t_mid, t_lo = _split3(total)
        cross = _dot(later_pages, t_hi) + _dot(later_pages, t_mid) + _dot(later_pages, t_lo)
        carry_rows = jnp.concatenate([carry] * cp, axis=0)
        w = _from_blocks(jnp.exp(log_beta + later + cross + carry_rows), cp)
        w16 = jnp.concatenate([w, jnp.zeros_like(w)], axis=0).astype(BF16)
        acc = acc + _dot_nt(w16, v_t.astype(BF16))
        carry = carry + cross[:8, :1] + total[:8, :1]
        return carry, acc

    carry, acc = process(kbuf[slot], vbuf[slot], jnp.zeros((8, 1), F32), jnp.zeros((ROWS16, LANES), F32))

    def cond(c):
        j, carry, _ = c
        return jnp.logical_and(j < n_chunks, jnp.max(carry[:N_HEADS]) > SB_DEAD)

    def body(c):
        j, carry, acc = c
        copies = chunk_copies(s, j, kmore, vmore, more_sems.at[0], more_sems.at[1])
        for cpy in copies:
            cpy.start()
        for cpy in copies:
            cpy.wait()
        carry, acc = process(kmore[...], vmore[...], carry, acc)
        return j + 1, carry, acc

    _, _, acc = lax.while_loop(cond, body, (1, carry, acc))
    o_ref[...] = acc


def _sb_sample(q16, cache_k, cache_v, pt_flat, consts, *, layer, n_pages, cp):
    s = q16.shape[0]
    assert n_pages % cp == 0
    grid_spec = pltpu.PrefetchScalarGridSpec(
        num_scalar_prefetch=1, grid=(s,),
        in_specs=[pl.BlockSpec((None, ROWS16, LANES), lambda i, pt: (i, 0, 0)),
                  pl.BlockSpec((LANES, LANES), lambda i, pt: (0, 0)),
                  pl.BlockSpec(memory_space=pl.ANY), pl.BlockSpec(memory_space=pl.ANY)],
        out_specs=pl.BlockSpec((None, ROWS16, LANES), lambda i, pt: (i, 0, 0)),
        scratch_shapes=[pltpu.VMEM((2, LANES, cp * PAGE), F32), pltpu.VMEM((2, LANES, cp * PAGE), F32),
                        pltpu.VMEM((LANES, cp * PAGE), F32), pltpu.VMEM((LANES, cp * PAGE), F32),
                        pltpu.SemaphoreType.DMA((2, 2)), pltpu.SemaphoreType.DMA((2,))])
    return pl.pallas_call(
        functools.partial(_sb_sample_body, layer=layer, n_pages=n_pages, cp=cp),
        grid_spec=grid_spec,
        out_shape=jax.ShapeDtypeStruct((s, ROWS16, LANES), F32),
        compiler_params=_cparams(("arbitrary",)), name="sb_sample",
    )(pt_flat, q16, consts["suffix"], cache_k, cache_v)


def _moba_sample_body(pt_ref, q_ref, ql_ref, knew_ref, vnew_ref, kc_hbm, vc_hbm, o_ref, kbuf, vbuf, sems,
                      *, layer, n_pages):
    slot, drain = _paged_prefetch((kc_hbm, vc_hbm), (kbuf, vbuf), (sems.at[0], sems.at[1]), ("blocks", "blocks"),
                                  pt_ref, layer, n_pages)
    past = n_pages * PAGE
    nb = past // MB_BLOCK
    q = q_ref[...]
    kmean = jnp.sum(kbuf[slot], axis=2) * (1.0 / MB_BLOCK)
    km_hi, km_lo = _split2(kmean)
    gate = (_dot_nt(q, km_hi) + _dot_nt(ql_ref[...], km_hi) + _dot_nt(q, km_lo)) * (HEAD_DIM ** 0.5)
    _, picks, oks = _top3(gate, jnp.ones(gate.shape, jnp.bool_))

    head = lax.broadcasted_iota(jnp.int32, (ROWS16, 1), 0)
    slope = _alibi_slope(head)
    key_off = lax.broadcasted_iota(jnp.int32, (ROWS16, MB_BLOCK), 1)
    chosen = [[jnp.minimum(picks[i][j, 0].astype(jnp.int32), nb - 1) for j in range(N_HEADS)]
              for i in range(MB_TOPK)]

    tiles = []
    for i in range(MB_TOPK):
        tile = jnp.zeros((ROWS16, MB_BLOCK), F32)
        for j in range(N_HEADS):
            n = chosen[i][j]
            dist = (past - n * MB_BLOCK) - key_off
            s_ij = _dot(q, kbuf[slot, n].astype(BF16)) - slope * dist.astype(F32)
            tile = jnp.where(head == j, s_ij, tile)
        tiles.append(jnp.where(oks[i], tile, NEG_BIG))
    s_new = jnp.sum(q.astype(F32) * knew_ref[...], axis=1, keepdims=True)
    m = s_new
    for tile in tiles:
        m = jnp.maximum(m, jnp.max(tile, axis=1, keepdims=True))
    p_new = jnp.exp(s_new - m)
    l = p_new
    o = p_new * vnew_ref[...]
    for i in range(MB_TOPK):
        p = jnp.exp(tiles[i] - m)
        l = l + jnp.sum(p, axis=1, keepdims=True)
        for j in range(N_HEADS):
            p_j = jnp.where(head == j, p, 0.0).astype(BF16)
            o = o + _dot_nt(p_j, vbuf[slot, chosen[i][j]].astype(BF16))
    o_ref[...] = o / l
    drain()


def _moba_sample(q16, q16_lo, k_new, v_new, cache_k, cache_v, pt_flat, *, layer, n_pages):
    s = q16.shape[0]
    nb = n_pages * PAGE // MB_BLOCK
    assert nb >= MB_TOPK and n_pages * PAGE % MB_BLOCK == 0
    grid_spec = pltpu.PrefetchScalarGridSpec(
        num_scalar_prefetch=1, grid=(s,),
        in_specs=[pl.BlockSpec((None, ROWS16, LANES), lambda i, pt: (i, 0, 0)),
                  pl.BlockSpec((None, ROWS16, LANES), lambda i, pt: (i, 0, 0)),
                  pl.BlockSpec((None, 1, LANES), lambda i, pt: (i, 0, 0)),
                  pl.BlockSpec((None, 1, LANES), lambda i, pt: (i, 0, 0)),
                  pl.BlockSpec(memory_space=pl.ANY), pl.BlockSpec(memory_space=pl.ANY)],
        out_specs=pl.BlockSpec((None, ROWS16, LANES), lambda i, pt: (i, 0, 0)),
        scratch_shapes=[pltpu.VMEM((2, nb, LANES, MB_BLOCK), F32), pltpu.VMEM((2, nb, LANES, MB_BLOCK), F32),
                        pltpu.SemaphoreType.DMA((2, 2))])
    return pl.pallas_call(
        functools.partial(_moba_sample_body, layer=layer, n_pages=n_pages),
        grid_spec=grid_spec,
        out_shape=jax.ShapeDtypeStruct((s, ROWS16, LANES), F32),
        compiler_params=_cparams(("arbitrary",)), name="moba_sample",
    )(pt_flat, q16, q16_lo, k_new, v_new, cache_k, cache_v)


def _mla_sample_body(pt_ref, ql_ref, qp_ref, cnew_ref, pnew_ref, wuv_ref, ckv_hbm, kpe_hbm, o_ref,
                     cbuf, pbuf, sems, *, layer, n_pages):
    slot, drain = _paged_prefetch((ckv_hbm, kpe_hbm), (cbuf, pbuf), (sems.at[0], sems.at[1]), ("rows", "lanes"),
                                  pt_ref, layer, n_pages)
    ql = ql_ref[...]
    qp = qp_ref[...]
    cb = cbuf[slot].astype(BF16)
    s = _dot_nt(ql, cb) + _dot(qp, pbuf[slot].astype(BF16))
    s_new = (jnp.sum(ql.astype(F32) * cnew_ref[...], axis=1, keepdims=True)
             + jnp.sum(qp.astype(F32) * pnew_ref[...], axis=1, keepdims=True))
    m = jnp.maximum(jnp.max(s, axis=1, keepdims=True), s_new)
    p = jnp.exp(s - m)
    p_new = jnp.exp(s_new - m)
    l = jnp.sum(p, axis=1, keepdims=True) + p_new
    o_lat = (_dot(p.astype(BF16), cb) + p_new * cnew_ref[...]) / l
    o_ref[...] = _dot(o_lat.astype(BF16), wuv_ref[...])
    drain()


def _mla_sample(ql16, qp16, ckv_new, kpe_new, wuv_flat, cache_ckv, cache_kpe, pt_flat, *, layer, n_pages):
    s = ql16.shape[0]
    t = n_pages * PAGE
    grid_spec = pltpu.PrefetchScalarGridSpec(
        num_scalar_prefetch=1, grid=(s,),
        in_specs=[pl.BlockSpec((None, ROWS16, LANES), lambda i, pt: (i, 0, 0)),
                  pl.BlockSpec((None, ROWS16, MLA_ROPE), lambda i, pt: (i, 0, 0)),
                  pl.BlockSpec((None, 1, LANES), lambda i, pt: (i, 0, 0)),
                  pl.BlockSpec((None, 1, MLA_ROPE), lambda i, pt: (i, 0, 0)),
                  pl.BlockSpec((MLA_KV_RANK, N_HEADS * MLA_V), lambda i, pt: (0, 0)),
                  pl.BlockSpec(memory_space=pl.ANY), pl.BlockSpec(memory_space=pl.ANY)],
        out_specs=pl.BlockSpec((None, ROWS16, N_HEADS * MLA_V), lambda i, pt: (i, 0, 0)),
        scratch_shapes=[pltpu.VMEM((2, t, LANES), F32), pltpu.VMEM((2, MLA_ROPE, t), F32),
                        pltpu.SemaphoreType.DMA((2, 2))])
    return pl.pallas_call(
        functools.partial(_mla_sample_body, layer=layer, n_pages=n_pages),
        grid_spec=grid_spec,
        out_shape=jax.ShapeDtypeStruct((s, ROWS16, N_HEADS * MLA_V), F32),
        compiler_params=_cparams(("arbitrary",)), name="mla_sample",
    )(pt_flat, ql16, qp16, ckv_new, kpe_new, wuv_flat, cache_ckv, cache_kpe)


def _rope_tables(pos):
    half = MLA_ROPE // 2
    inv = jnp.power(ROPE_THETA, -jnp.arange(half, dtype=F32) / half)
    ang = pos.astype(F32)[:, None] * inv
    cos, sin = jnp.cos(ang), jnp.sin(ang)
    cos32 = jnp.concatenate([cos, cos], axis=1)
    sin32 = jnp.concatenate([-sin, sin], axis=1)
    reps = LANES // MLA_ROPE
    return jnp.tile(cos32, (1, reps)), jnp.tile(sin32, (1, reps))


def _block_diag(blocks):
    n, r, c = blocks.shape
    eye = jnp.eye(n, dtype=blocks.dtype)
    return (eye[:, None, :, None] * blocks[:, :, None, :]).reshape(n * r, n * c)


def _layer_weights(l, norm1_g, w_in, b_gate, w_branch, w_out, q_norm_g, kv_norm_g, w_uq, w_uk, w_uv,
                   conv_w, conv_b, w_rg_a, b_rg_a, w_rg_x, b_rg_x, lru_lambda, norm2_g, w_mlp1, w_mlp2):
    w = w_in[l]
    cq_pad = 256 - MLA_Q_RANK
    w_proj = jnp.concatenate([
        w[:, 0:1024],
        jnp.pad(w[:, 1024:1216], ((0, 0), (0, cq_pad))),
        w[:, 1216:1344],
        jnp.tile(w[:, 1344:1376], (1, LANES // MLA_ROPE)),
        w[:, 1376:1888]], axis=1).astype(BF16)
    w_moba = w[:, 512:896]
    w_proj_lo = (w_moba - w_moba.astype(BF16).astype(F32)).astype(BF16)
    uq = jnp.pad(w_uq[l], ((0, cq_pad), (0, 0), (0, 0)))
    uk = jnp.transpose(w_uk[l], (1, 2, 0))
    return {
        "g1": norm1_g[l][None, :], "w_proj": w_proj, "w_proj_lo": w_proj_lo,
        "qg": jnp.pad(q_norm_g[l], (0, cq_pad))[None, :], "kvg": kv_norm_g[l][None, :],
        "wuq_nope": uq[:, :, :MLA_NOPE].reshape(256, N_HEADS * MLA_NOPE).astype(BF16),
        "wuq_pe": uq[:, :, MLA_NOPE:].reshape(256, N_HEADS * MLA_ROPE).astype(BF16),
        "wuk_bd": _block_diag(uk).astype(BF16),
        "wuv_heads": jnp.transpose(w_uv[l], (1, 0, 2)).astype(BF16),
        "wuv_flat": w_uv[l].reshape(MLA_KV_RANK, N_HEADS * MLA_V).astype(BF16),
        "conv_w": conv_w[l], "conv_b": conv_b[l][None, :],
        "w_rg_a": _block_diag(w_rg_a[l]).astype(BF16), "b_rg_a": b_rg_a[l][None, :],
        "w_rg_x": _block_diag(w_rg_x[l]).astype(BF16), "b_rg_x": b_rg_x[l][None, :],
        "lam": lru_lambda[l][None, :],
        "w_gl": w[:, 1888:].astype(BF16), "b_gate": b_gate[l].reshape(1, N_BRANCH * D_MODEL),
        "w_branch": w_branch[l].astype(BF16), "w_out": w_out[l].astype(BF16),
        "g2": norm2_g[l][None, :], "w_mlp1": w_mlp1[l].astype(BF16), "w_mlp2": w_mlp2[l].astype(BF16),
    }


def _decode_consts():
    j = jnp.arange(LANES)
    return {"suffix": (j[:, None] > j[None, :]).astype(BF16)}


def _keys_on_lanes(cache):
    d, n = cache.shape[:2]
    moved = jnp.moveaxis(cache, 2, -1)
    return moved.reshape(d, n, -1, PAGE)


def _rows16(x):
    return jnp.pad(x, ((0, 0), (0, ROWS16 - x.shape[1]), (0, 0)))


def _pick_head_lanes(o16):
    parts = [o16[:, j, (j // 2) * HEAD_DIM:(j // 2 + 1) * HEAD_DIM] for j in range(N_HEADS)]
    return jnp.concatenate(parts, axis=1)


def kernel(x_prompt, x_sample, cache_sb_k, cache_sb_v, cache_moba_k, cache_moba_v, cache_mla_ckv, cache_mla_kpe,
           state_lru_h, state_lru_conv, page_table, norm1_g, w_in, b_gate, w_branch, w_out, q_norm_g, kv_norm_g,
           w_uq, w_uk, w_uv, conv_w, conv_b, w_rg_a, b_rg_a, w_rg_x, b_rg_x, lru_lambda, norm2_g, w_mlp1, w_mlp2,
           final_norm_g):
    batch, seq, _ = x_prompt.shape
    n_dec = x_sample.shape[0]
    depth = w_in.shape[0]
    n_pages = page_table.shape[1]
    past = n_pages * PAGE
    assert x_sample.shape[1] == 1 and seq % 512 == 0 and past % MB_BLOCK == 0

    tm_p = 512
    tm_s = n_dec
    nb = seq // MB_BLOCK
    nb_pad = max(LANES, nb)

    tab_p = _rope_tables(jnp.arange(seq, dtype=jnp.int32))
    tab_s = _rope_tables(jnp.full((n_dec,), past, jnp.int32))
    consts = _decode_consts()
    pt_flat = page_table.reshape(-1)
    gf = final_norm_g[None, :]

    c_sb_k, c_sb_v = _keys_on_lanes(cache_sb_k), _keys_on_lanes(cache_sb_v)
    c_mb_k, c_mb_v = _keys_on_lanes(cache_moba_k), _keys_on_lanes(cache_moba_v)
    c_kpe = _keys_on_lanes(cache_mla_kpe)

    xp = x_prompt.reshape(batch * seq, D_MODEL)
    xs = x_sample.reshape(n_dec, D_MODEL)
    p_states = [[] for _ in range(8)]
    s_states = [[] for _ in range(8)]

    for l in range(depth):
        lw = _layer_weights(l, norm1_g, w_in, b_gate, w_branch, w_out, q_norm_g, kv_norm_g, w_uq, w_uk, w_uv,
                            conv_w, conv_b, w_rg_a, b_rg_a, w_rg_x, b_rg_x, lru_lambda, norm2_g, w_mlp1, w_mlp2)
        final = l == depth - 1

        (qa, _, _, ka, va, kb, vb, kab, vab, kbb, _, _, _, kc, ckv, kpe, xrg,
         kmean, qc_t, ckv_t, qb_t, qb_t_lo, vb_t) = _proj(xp, tab_p, lw, tm=tm_p, prompt=True)
        kmean = jnp.pad(kmean.reshape(batch, nb, LANES), ((0, 0), (0, nb_pad - nb), (0, 0)))
        o_a = _sb_prompt(qa, kab, vab, batch=batch, seq=seq, tq=SB_TQ, tk=SB_TK)
        o_b = _moba_prompt(qb_t, qb_t_lo, kbb, vb_t, kmean, batch=batch, seq=seq, tq=256)
        o_c = _mla_prompt(qc_t, kc, ckv_t, lw["wuv_heads"], batch=batch, seq=seq, tq=MLA_TQ, tk=MLA_TK)
        o_d, h_last = _lru_prompt(xrg, lw, batch=batch, seq=seq, tt=512)
        xp = _mlp(_merge(xp, (o_a, o_b, o_c, o_d), lw, tm=tm_p), lw, gf, tm=tm_p, final=final)
        new_p = (ka.reshape(batch, seq, N_KV, HEAD_DIM), va.reshape(batch, seq, N_KV, HEAD_DIM),
                 kb.reshape(batch, seq, N_KV, HEAD_DIM), vb.reshape(batch, seq, N_KV, HEAD_DIM),
                 ckv.reshape(batch, seq, MLA_KV_RANK), kpe.reshape(batch, seq, MLA_ROPE),
                 h_last.reshape(batch, LRU_WIDTH),
                 xrg.reshape(batch, seq, 512)[:, seq - (CONV_W - 1):, :LRU_WIDTH])
        for lst, a in zip(p_states, new_p):
            lst.append(a)

        (qa, qb, qbl, ka, va, kb, vb, _, _, _, _, qc, qpe, _, ckv, kpe, xrg) = _proj(
            xs, tab_s, lw, tm=tm_s, prompt=False)
        qa16 = _rows16(qa.reshape(n_dec, N_HEADS, LANES))
        qb16 = _rows16(qb.reshape(n_dec, N_HEADS, LANES))
        qbl16 = _rows16(qbl.reshape(n_dec, N_HEADS, LANES))
        ql16 = _rows16(qc.reshape(n_dec, N_HEADS, 256)[:, :, :LANES])
        qp16 = _rows16(qpe.reshape(n_dec, N_HEADS, MLA_ROPE))
        o_a = _pick_head_lanes(_sb_sample(qa16, c_sb_k, c_sb_v, pt_flat, consts, layer=l, n_pages=n_pages,
                                          cp=SB_CHUNK_PAGES))
        o_b = _pick_head_lanes(_moba_sample(qb16, qbl16, kb[:, None, :], vb[:, None, :], c_mb_k, c_mb_v, pt_flat,
                                            layer=l, n_pages=n_pages))
        o_c16 = _mla_sample(ql16, qp16, ckv[:, None, :], kpe[:, None, :], lw["wuv_flat"], cache_mla_ckv,
                            c_kpe, pt_flat, layer=l, n_pages=n_pages)
        o_c = jnp.concatenate([o_c16[:, h, h * MLA_V:(h + 1) * MLA_V] for h in range(N_HEADS)], axis=1)
        buf0 = state_lru_conv[l]
        o_d, h_new = _lru_sample(xrg, jnp.transpose(buf0, (1, 0, 2)), state_lru_h[l], lw)
        outs = (o_a.astype(BF16), o_b.astype(BF16), o_c.astype(BF16), o_d)
        xs = _mlp(_merge(xs, outs, lw, tm=tm_s), lw, gf, tm=tm_s, final=final)
        new_s = (ka.reshape(n_dec, 1, N_KV, HEAD_DIM), va.reshape(n_dec, 1, N_KV, HEAD_DIM),
                 kb.reshape(n_dec, 1, N_KV, HEAD_DIM), vb.reshape(n_dec, 1, N_KV, HEAD_DIM),
                 ckv.reshape(n_dec, 1, MLA_KV_RANK), kpe.reshape(n_dec, 1, MLA_ROPE), h_new,
                 jnp.concatenate([buf0[:, 1:], xrg[:, None, :LRU_WIDTH]], axis=1))
        for lst, a in zip(s_states, new_s):
            lst.append(a)

    y_prompt = xp.reshape(batch, seq, D_MODEL)
    y_sample = xs.reshape(n_dec, 1, D_MODEL)
    return (y_prompt, y_sample, *[jnp.stack(s) for s in p_states], *[jnp.stack(s) for s in s_states])
```
